```python
import math
import jax, jax.numpy as jnp
from jax import lax
import numpy as np

D_MODEL = 1024
BATCH = 16
SEQ = 4096
DEPTH = 2

GRID_W = 64
HEAD_DIM = D_MODEL // 16
GROUP_W = 4 * HEAD_DIM
MIX_WIDTH = 4 * GROUP_W
N_HEADS_A = 4
N_KV_A = 2
GQA_GROUP = N_HEADS_A // N_KV_A
N_HEADS_B = 4
MLA_Q_RANK = 3 * HEAD_DIM
MLA_KV_RANK = 2 * HEAD_DIM
MLA_NOPE = HEAD_DIM
MLA_ROPE = HEAD_DIM // 2
MLA_V = HEAD_DIM
N_HEADS_C = 4
DIFF_QK = HEAD_DIM // 2
DIFF_V = HEAD_DIM
N_HEADS_D = 4
NA_ROWS_MAX = 8
NA_COLS = 16
D_FF = 4 * D_MODEL
Q_BLOCK = 128
ROPE_BASE = 10000.0
EPS = 1e-6

A_COLS = (N_HEADS_A + 2 * N_KV_A) * HEAD_DIM
B_COLS = MLA_Q_RANK + MLA_KV_RANK + MLA_ROPE
C_COLS = 3 * GROUP_W
D_COLS = 3 * GROUP_W
IN_COLS = A_COLS + B_COLS + C_COLS + D_COLS
SPLITS = (A_COLS, A_COLS + B_COLS, A_COLS + B_COLS + C_COLS)

kernel_name = 'hybrid_parallel_heads_encoder'


def rms_norm(x, g):
    xf = x.astype(jnp.float32)
    y = xf * lax.rsqrt(jnp.mean(xf * xf, axis=-1, keepdims=True) + EPS)
    return (y * g.astype(jnp.float32)).astype(x.dtype)


def rope_angles(pos, dim):
    inv = ROPE_BASE ** (-jnp.arange(0, dim, 2, dtype=jnp.float32) / dim)
    return pos.astype(jnp.float32)[:, None] * inv[None, :]


def apply_rope(x, ang):
    x1, x2 = jnp.split(x, 2, axis=-1)
    cos = jnp.cos(ang).astype(x.dtype)
    sin = jnp.sin(ang).astype(x.dtype)
    return jnp.concatenate([x1 * cos - x2 * sin, x1 * sin + x2 * cos], axis=-1)


def axial_rope(x, ang_row, ang_col):
    xr, xc = jnp.split(x, 2, axis=-1)
    return jnp.concatenate([apply_rope(xr, ang_row), apply_rope(xc, ang_col)], axis=-1)


def alibi_slopes(n_heads):
    return jnp.exp2(-8.0 * jnp.arange(1, n_heads + 1, dtype=jnp.float32) / n_heads)


def neighbourhood_index(n_tokens):
    rows = n_tokens // GRID_W
    kr = min(NA_ROWS_MAX, rows)
    kc = min(NA_COLS, GRID_W)
    t = jnp.arange(n_tokens, dtype=jnp.int32)
    r, c = t // GRID_W, t % GRID_W
    r0 = jnp.clip(r - kr // 2, 0, rows - kr)
    c0 = jnp.clip(c - kc // 2, 0, GRID_W - kc)
    key_r = r0[:, None, None] + jnp.arange(kr, dtype=jnp.int32)[None, :, None]
    key_c = c0[:, None, None] + jnp.arange(kc, dtype=jnp.int32)[None, None, :]
    idx = (key_r * GRID_W + key_c).reshape(n_tokens, kr * kc)
    dr = key_r - r[:, None, None] + (NA_ROWS_MAX - 1)
    dc = key_c - c[:, None, None] + (NA_COLS - 1)
    rel = (dr * (2 * NA_COLS - 1) + dc).reshape(n_tokens, kr * kc)
    return idx, rel


def sweep_query_blocks(block_fn, n_tokens):
    out = lax.map(block_fn, jnp.arange(n_tokens // Q_BLOCK))
    return jnp.moveaxis(out, 0, 1).reshape(out.shape[1], n_tokens, out.shape[-1])


def gqa_axial(u, q_gain, k_gain, ang_row, ang_col):
    b, s, _ = u.shape
    q, k, v = jnp.split(u, [N_HEADS_A * HEAD_DIM, (N_HEADS_A + N_KV_A) * HEAD_DIM], axis=-1)
    q = q.reshape(b, s, N_KV_A, GQA_GROUP, HEAD_DIM).transpose(0, 2, 3, 1, 4)
    k = k.reshape(b, s, N_KV_A, HEAD_DIM).transpose(0, 2, 1, 3)
    v = v.reshape(b, s, N_KV_A, HEAD_DIM).transpose(0, 2, 1, 3)
    q = axial_rope(rms_norm(q, q_gain), ang_row, ang_col)
    k = axial_rope(rms_norm(k, k_gain), ang_row, ang_col)
    scale = HEAD_DIM ** -0.5

    def block(i):
        qb = lax.dynamic_slice_in_dim(q, i * Q_BLOCK, Q_BLOCK, axis=3)
        sc = jnp.einsum('bkgqd,bksd->bkgqs', qb, k, preferred_element_type=jnp.float32) * scale
        p = jax.nn.softmax(sc, axis=-1).astype(v.dtype)
        o = jnp.einsum('bkgqs,bksd->bqkgd', p, v)
        return o.reshape(b, Q_BLOCK, N_HEADS_A * HEAD_DIM)

    return sweep_query_blocks(block, s)


def mla(u, cq_gain, ckv_gain, w_uq, w_ukv, ang_seq):
    b, s, _ = u.shape
    c_q, c_kv, k_rope = jnp.split(u, [MLA_Q_RANK, MLA_Q_RANK + MLA_KV_RANK], axis=-1)
    q = (rms_norm(c_q, cq_gain) @ w_uq).reshape(b, s, N_HEADS_B, MLA_NOPE + MLA_ROPE).transpose(0, 2, 1, 3)
    kv = (rms_norm(c_kv, ckv_gain) @ w_ukv).reshape(b, s, N_HEADS_B, MLA_NOPE + MLA_V).transpose(0, 2, 1, 3)
    q_nope, q_rope = jnp.split(q, [MLA_NOPE], axis=-1)
    k_nope, v = jnp.split(kv, [MLA_NOPE], axis=-1)
    q_rope = apply_rope(q_rope, ang_seq)
    k_rope = apply_rope(k_rope, ang_seq)
    scale = (MLA_NOPE + MLA_ROPE) ** -0.5

    def block(i):
        qn = lax.dynamic_slice_in_dim(q_nope, i * Q_BLOCK, Q_BLOCK, axis=2)
        qr = lax.dynamic_slice_in_dim(q_rope, i * Q_BLOCK, Q_BLOCK, axis=2)
        sc = (jnp.einsum('bhqd,bhsd->bhqs', qn, k_nope, preferred_element_type=jnp.float32)
              + jnp.einsum('bhqr,bsr->bhqs', qr, k_rope, preferred_element_type=jnp.float32)) * scale
        p = jax.nn.softmax(sc, axis=-1).astype(v.dtype)
        o = jnp.einsum('bhqs,bhsd->bqhd', p, v)
        return o.reshape(b, Q_BLOCK, N_HEADS_B * MLA_V)

    return sweep_query_blocks(block, s)


def diff_attention(u, lq1, lk1, lq2, lk2, g_head, slopes, lambda_init):
    b, s, _ = u.shape
    q, k, v = jnp.split(u, [GROUP_W, 2 * GROUP_W], axis=-1)
    q = q.reshape(b, s, N_HEADS_C, 2, DIFF_QK).transpose(0, 2, 3, 1, 4)
    k = k.reshape(b, s, N_HEADS_C, 2, DIFF_QK).transpose(0, 2, 3, 1, 4)
    v = v.reshape(b, s, N_HEADS_C, DIFF_V).transpose(0, 2, 1, 3)
    f32 = jnp.float32
    lam = (jnp.exp(jnp.sum(lq1.astype(f32) * lk1.astype(f32)))
           - jnp.exp(jnp.sum(lq2.astype(f32) * lk2.astype(f32))) + lambda_init)
    t = jnp.arange(s, dtype=jnp.int32)
    scale = DIFF_QK ** -0.5

    def block(i):
        qb = lax.dynamic_slice_in_dim(q, i * Q_BLOCK, Q_BLOCK, axis=3)
        qpos = i * Q_BLOCK + jnp.arange(Q_BLOCK, dtype=jnp.int32)
        dist = jnp.abs(qpos[:, None] - t[None, :]).astype(f32)
        bias = -slopes[:, None, None, None] * dist
        sc = jnp.einsum('bhmqd,bhmsd->bhmqs', qb, k, preferred_element_type=f32) * scale + bias
        p = jax.nn.softmax(sc, axis=-1)
        a = (p[:, :, 0] - lam * p[:, :, 1]).astype(v.dtype)
        o = jnp.einsum('bhqs,bhsd->bqhd', a, v)
        o = rms_norm(o, g_head) * (1.0 - lambda_init)
        return o.reshape(b, Q_BLOCK, N_HEADS_C * DIFF_V)

    return sweep_query_blocks(block, s)


def neighbourhood_attention(u, rel_bias, na_idx, na_rel):
    b, s, _ = u.shape
    q, k, v = jnp.split(u, [GROUP_W, 2 * GROUP_W], axis=-1)
    q = q.reshape(b, s, N_HEADS_D, HEAD_DIM)
    k = k.reshape(b, s, N_HEADS_D, HEAD_DIM)
    v = v.reshape(b, s, N_HEADS_D, HEAD_DIM)
    bias_flat = rel_bias.reshape(N_HEADS_D, -1).astype(jnp.float32)
    scale = HEAD_DIM ** -0.5

    def block(i):
        qb = lax.dynamic_slice_in_dim(q, i * Q_BLOCK, Q_BLOCK, axis=1)
        idx = lax.dynamic_slice_in_dim(na_idx, i * Q_BLOCK, Q_BLOCK, axis=0)
        rel = lax.dynamic_slice_in_dim(na_rel, i * Q_BLOCK, Q_BLOCK, axis=0)
        kb = jnp.take(k, idx, axis=1)
        vb = jnp.take(v, idx, axis=1)
        sc = (jnp.einsum('bqhd,bqnhd->bhqn', qb, kb, preferred_element_type=jnp.float32) * scale
              + jnp.take(bias_flat, rel, axis=1))
        p = jax.nn.softmax(sc, axis=-1).astype(v.dtype)
        o = jnp.einsum('bhqn,bqnhd->bqhd', p, vb)
        return o.reshape(b, Q_BLOCK, N_HEADS_D * HEAD_DIM)

    return sweep_query_blocks(block, s)


def setup_inputs(seed: int = 0) -> dict:
    key = jax.random.key(seed)
    ks = jax.random.split(key, 26)
    L = DEPTH

    def dense(k, fan_in, fan_out):
        return jax.random.normal(k, (L, fan_in, fan_out), jnp.float32) * fan_in ** -0.5

    def gain(k, dim):
        return 1.0 + 0.02 * jax.random.normal(k, (L, dim), jnp.float32)

    def small(k, shape, sc):
        return sc * jax.random.normal(k, shape, jnp.float32)

    return {
        'x': jax.random.normal(ks[0], (BATCH, SEQ, D_MODEL), jnp.float32),
        'norm_mix_pre': gain(ks[1], D_MODEL),
        'norm_mix_post': gain(ks[2], D_MODEL),
        'norm_mlp_pre': gain(ks[3], D_MODEL),
        'norm_mlp_post': gain(ks[4], D_MODEL),
        'w_in': dense(ks[5], D_MODEL, IN_COLS),
        'a_q_norm': gain(ks[6], HEAD_DIM),
        'a_k_norm': gain(ks[7], HEAD_DIM),
        'b_cq_norm': gain(ks[8], MLA_Q_RANK),
        'b_ckv_norm': gain(ks[9], MLA_KV_RANK),
        'b_w_uq': dense(ks[10], MLA_Q_RANK, N_HEADS_B * (MLA_NOPE + MLA_ROPE)),
        'b_w_ukv': dense(ks[11], MLA_KV_RANK, N_HEADS_B * (MLA_NOPE + MLA_V)),
        'c_lambda_q1': small(ks[12], (L, DIFF_QK), 0.1),
        'c_lambda_k1': small(ks[13], (L, DIFF_QK), 0.1),
        'c_lambda_q2': small(ks[14], (L, DIFF_QK), 0.1),
        'c_lambda_k2': small(ks[15], (L, DIFF_QK), 0.1),
        'd_rel_bias': small(ks[16], (L, N_HEADS_D, 2 * NA_ROWS_MAX - 1, 2 * NA_COLS - 1), 0.1),
        'g_out_a': gain(ks[17], GROUP_W),
        'g_out_b': gain(ks[18], GROUP_W),
        'g_out_c': gain(ks[19], DIFF_V),
        'g_out_d': gain(ks[20], GROUP_W),
        'w_out': dense(ks[21], MIX_WIDTH, D_MODEL),
        'w_up': dense(ks[22], D_MODEL, D_FF),
        'w_down': dense(ks[23], D_FF, D_MODEL),
    }


def reference(x, norm_mix_pre, norm_mix_post, norm_mlp_pre, norm_mlp_post, w_in,
              a_q_norm, a_k_norm, b_cq_norm, b_ckv_norm, b_w_uq, b_w_ukv,
              c_lambda_q1, c_lambda_k1, c_lambda_q2, c_lambda_k2, d_rel_bias,
              g_out_a, g_out_b, g_out_c, g_out_d, w_out, w_up, w_down):
    n_tokens = x.shape[1]
    t = jnp.arange(n_tokens, dtype=jnp.int32)
    ang_row = rope_angles(t // GRID_W, HEAD_DIM // 2)
    ang_col = rope_angles(t % GRID_W, HEAD_DIM // 2)
    ang_seq = rope_angles(t, MLA_ROPE)
    slopes = alibi_slopes(N_HEADS_C)
    na_idx, na_rel = neighbourhood_index(n_tokens)

    for l in range(DEPTH):
        h = rms_norm(x, norm_mix_pre[l])
        u = h @ w_in[l]
        ua, ub, uc, ud = jnp.split(u, SPLITS, axis=-1)
        oa = rms_norm(gqa_axial(ua, a_q_norm[l], a_k_norm[l], ang_row, ang_col), g_out_a[l])
        ob = rms_norm(mla(ub, b_cq_norm[l], b_ckv_norm[l], b_w_uq[l], b_w_ukv[l], ang_seq), g_out_b[l])
        lambda_init = 0.8 - 0.6 * math.exp(-0.3 * l)
        oc = diff_attention(uc, c_lambda_q1[l], c_lambda_k1[l], c_lambda_q2[l], c_lambda_k2[l],
                            g_out_c[l], slopes, lambda_init)
        od = rms_norm(neighbourhood_attention(ud, d_rel_bias[l], na_idx, na_rel), g_out_d[l])
        mix = jnp.concatenate([oa, ob, oc, od], axis=-1) @ w_out[l]
        x = x + rms_norm(mix, norm_mix_post[l])
        h = rms_norm(x, norm_mlp_pre[l])
        f = jnp.square(jax.nn.relu(h @ w_up[l])) @ w_down[l]
        x = x + rms_norm(f, norm_mlp_post[l])
    return x
```

```python
import functools
import math

import numpy as np
import jax
import jax.numpy as jnp
from jax import lax
from jax.experimental import pallas as pl
from jax.experimental.pallas import tpu as pltpu

GRID_W = 64
HEAD_DIM = 64
GROUP_W = 256
N_KV_A = 2
MLA_Q_RANK = 192
MLA_KV_RANK = 128
MLA_NOPE = 64
MLA_ROPE = 32
DIFF_QK = 32
NA_ROWS = 8
NA_COLS = 16
ROPE_BASE = 10000.0
EPS = 1e-6
LOG2E = math.log2(math.e)
NEG_BIG = -1e30

A_COLS = 512
B_COLS = MLA_Q_RANK + MLA_KV_RANK + MLA_ROPE
C_COLS = 768
D_COLS = 768
OFF_B = A_COLS
OFF_C = OFF_B + B_COLS
OFF_D = OFF_C + C_COLS
IN_COLS = OFF_D + D_COLS

TOKEN_TILE = 512
Q_TILE = 512
KV_CHUNK = 512
NA_Q_TILE = 256
NA_SLAB_ROWS = 12
VMEM_LIMIT = 56 * 1024 * 1024

F32 = jnp.float32
BF16 = jnp.bfloat16


def _dot(a, b):
    return jnp.dot(a, b, preferred_element_type=F32)


def _rms_rows(x, gain):
    ms = jnp.mean(x * x, axis=0, keepdims=True)
    return x * lax.rsqrt(ms + EPS) * gain


def _rope_rows(x, cos, sin):
    n = cos.shape[0]
    x1, x2 = x[:n], x[n:]
    return x1 * cos - x2 * sin, x1 * sin + x2 * cos


def _inproj_kernel(x_ref, gpre_ref, w_ref, aq_ref, ak_ref, cqg_ref, ckvg_ref, wuq_ref, wukv_ref,
                   cr_ref, sr_ref, cc_ref, sc_ref, cs_ref, ss_ref,
                   qa_ref, ka_ref, va_ref, qb_ref, kb_ref, vb_ref,
                   qc_ref, kc_ref, vc_ref, qd_ref, kd_ref, vd_ref):
    x = x_ref[0]
    h = _rms_rows(x, gpre_ref[...]).astype(BF16)
    cr, sr, cc, sc = cr_ref[...], sr_ref[...], cc_ref[...], sc_ref[...]
    cs, ss = cs_ref[...], ss_ref[...]
    tm = x.shape[1]

    def axial(t):
        r1, r2 = _rope_rows(t[0:32], cr, sr)
        c1, c2 = _rope_rows(t[32:64], cc, sc)
        return r1, r2, c1, c2

    ua = _dot(w_ref[0:A_COLS, :], h)
    qscale = HEAD_DIM ** -0.5 * LOG2E
    for hd in range(4):
        t = _rms_rows(ua[64 * hd:64 * hd + 64], aq_ref[...])
        for j, piece in enumerate(axial(t)):
            qa_ref[0, 64 * hd + 16 * j:64 * hd + 16 * j + 16, :] = (piece * qscale).astype(BF16)
    kparts = []
    for g in range(N_KV_A):
        t = _rms_rows(ua[256 + 64 * g:256 + 64 * g + 64], ak_ref[...])
        kparts.extend(axial(t))
    ka_ref[0] = jnp.concatenate(kparts, axis=0).T.astype(BF16)
    va_ref[0] = ua[384:512].astype(BF16)

    ub = _dot(w_ref[OFF_B:OFF_B + B_COLS, :], h)
    cqn = _rms_rows(ub[0:MLA_Q_RANK], cqg_ref[...]).astype(BF16)
    ckvn = _rms_rows(ub[MLA_Q_RANK:MLA_Q_RANK + MLA_KV_RANK], ckvg_ref[...]).astype(BF16)
    kr1, kr2 = _rope_rows(ub[MLA_Q_RANK + MLA_KV_RANK:B_COLS], cs, ss)
    qb = _dot(wuq_ref[...], cqn)
    kvb = _dot(wukv_ref[...], ckvn)
    bscale = (MLA_NOPE + MLA_ROPE) ** -0.5 * LOG2E
    zeros32 = jnp.zeros((32, tm), F32)
    for hd in range(4):
        base = 96 * hd
        q1, q2 = _rope_rows(qb[base + 64:base + 96], cs, ss)
        qb_ref[0, 128 * hd:128 * hd + 64, :] = (qb[base:base + 64] * bscale).astype(BF16)
        qb_ref[0, 128 * hd + 64:128 * hd + 80, :] = (q1 * bscale).astype(BF16)
        qb_ref[0, 128 * hd + 80:128 * hd + 96, :] = (q2 * bscale).astype(BF16)
        qb_ref[0, 128 * hd + 96:128 * hd + 128, :] = jnp.zeros((32, tm), BF16)
        kfull = jnp.concatenate([kvb[128 * hd:128 * hd + 64], kr1, kr2, zeros32], axis=0)
        kb_ref[0, hd] = kfull.T.astype(BF16)
        vb_ref[0, 64 * hd:64 * hd + 64, :] = kvb[128 * hd + 64:128 * hd + 128].astype(BF16)

    uc = _dot(w_ref[OFF_C:OFF_C + C_COLS, :], h)
    qc_ref[0] = (uc[0:256] * (DIFF_QK ** -0.5 * LOG2E)).astype(BF16)
    for p in range(2):
        kc_ref[0, p] = uc[256 + 128 * p:256 + 128 * p + 128].T.astype(BF16)
    vc_ref[0] = uc[512:768].astype(BF16)

    ud = _dot(w_ref[OFF_D:OFF_D + D_COLS, :], h)
    qd_ref[0] = (ud[0:256] * (HEAD_DIM ** -0.5 * LOG2E)).astype(BF16)
    for p in range(2):
        kd_ref[0, p] = ud[256 + 128 * p:256 + 128 * p + 128].T.astype(BF16)
    vd_ref[0] = ud[512:768].astype(BF16)


def _const_spec(shape):
    nd = len(shape)
    return pl.BlockSpec(shape, lambda *_: (0,) * nd, pipeline_mode=pl.Buffered(1))


def _inproj(xT, gpre, wT, aq, ak, cqg, ckvg, wuqT, wukvT, tables):
    B, D, S = xT.shape
    tm = min(TOKEN_TILE, S)
    grid = (B, S // tm)
    tab_spec = pl.BlockSpec((16, tm), lambda b, i: (0, i))
    rows_spec = lambda rows: pl.BlockSpec((1, rows, tm), lambda b, i: (b, 0, i))
    out_shape = (
        jax.ShapeDtypeStruct((B, 256, S), BF16),
        jax.ShapeDtypeStruct((B, S, 128), BF16),
        jax.ShapeDtypeStruct((B, 128, S), BF16),
        jax.ShapeDtypeStruct((B, 512, S), BF16),
        jax.ShapeDtypeStruct((B, 4, S, 128), BF16),
        jax.ShapeDtypeStruct((B, 256, S), BF16),
        jax.ShapeDtypeStruct((B, 256, S), BF16),
        jax.ShapeDtypeStruct((B, 2, S, 128), BF16),
        jax.ShapeDtypeStruct((B, 256, S), BF16),
        jax.ShapeDtypeStruct((B, 256, S), BF16),
        jax.ShapeDtypeStruct((B, 2, S, 128), BF16),
        jax.ShapeDtypeStruct((B, 256, S), BF16),
    )
    kspec = lambda n: pl.BlockSpec((1, n, tm, 128), lambda b, i: (b, 0, i, 0))
    out_specs = (
        rows_spec(256), pl.BlockSpec((1, tm, 128), lambda b, i: (b, i, 0)), rows_spec(128),
        rows_spec(512), kspec(4), rows_spec(256),
        rows_spec(256), kspec(2), rows_spec(256),
        rows_spec(256), kspec(2), rows_spec(256),
    )
    in_specs = [
        rows_spec(D), _const_spec(gpre.shape), _const_spec(wT.shape),
        _const_spec(aq.shape), _const_spec(ak.shape), _const_spec(cqg.shape), _const_spec(ckvg.shape),
        _const_spec(wuqT.shape), _const_spec(wukvT.shape),
    ] + [tab_spec] * 6
    return pl.pallas_call(
        _inproj_kernel, grid=grid, in_specs=in_specs, out_specs=out_specs, out_shape=out_shape,
        compiler_params=pltpu.CompilerParams(vmem_limit_bytes=VMEM_LIMIT),
        name="inproj",
    )(xT, gpre, wT, aq, ak, cqg, ckvg, wuqT, wukvT, *tables)


def _pad_rows(q32, slot, nslots):
    return jnp.concatenate([jnp.where(slot == j, q32, 0.0) for j in range(nslots)], axis=0)


def _online_step(s, vc, carry):
    m, l, acc = carry
    m_new = jnp.maximum(m, jnp.max(s, axis=0, keepdims=True))
    alpha = jnp.exp2(m - m_new)
    p = jnp.exp2(s - m_new)
    l = alpha * l + jnp.sum(p, axis=0, keepdims=True)
    acc = alpha * acc + _dot(vc, p.astype(BF16))
    return m_new, l, acc


def _softmax_init(dv, tq):
    return (jnp.full((1, tq), NEG_BIG, F32), jnp.zeros((1, tq), F32), jnp.zeros((dv, tq), F32))


def _attn_kernel(q_ref, k_ref, v_ref, o_ref, *, k_slots, nchunks, tk):
    q32 = q_ref[0].astype(F32)
    if k_slots > 1:
        q32 = _pad_rows(q32, pl.program_id(1) // 2, k_slots)
    qp = q32.astype(BF16)
    tq = qp.shape[1]
    k_at = (lambda c: k_ref[0, pl.ds(c, tk), :]) if len(k_ref.shape) == 3 else (
        lambda c: k_ref[0, 0, pl.ds(c, tk), :])

    def body(c, carry):
        off = pl.multiple_of(c * tk, tk)
        s = _dot(k_at(off), qp)
        return _online_step(s, v_ref[0, :, pl.ds(off, tk)], carry)

    m, l, acc = lax.fori_loop(0, nchunks, body, _softmax_init(v_ref.shape[1], tq))
    o_ref[0] = acc / l


def _attn_a(qT, k, vT):
    B, _, S = qT.shape
    tq, tk = min(Q_TILE, S), min(KV_CHUNK, S)
    return pl.pallas_call(
        functools.partial(_attn_kernel, k_slots=2, nchunks=S // tk, tk=tk),
        grid=(B, 4, S // tq),
        in_specs=[pl.BlockSpec((1, 64, tq), lambda b, h, i: (b, h, i)),
                  pl.BlockSpec((1, S, 128), lambda b, h, i: (b, 0, 0)),
                  pl.BlockSpec((1, 64, S), lambda b, h, i: (b, h // 2, 0))],
        out_specs=pl.BlockSpec((1, 64, tq), lambda b, h, i: (b, h, i)),
        out_shape=jax.ShapeDtypeStruct((B, 256, S), F32),
        compiler_params=pltpu.CompilerParams(vmem_limit_bytes=VMEM_LIMIT),
        name="attn_gqa",
    )(qT, k, vT)


def _attn_b(qT, k, vT):
    B, _, S = qT.shape
    tq, tk = min(Q_TILE, S), min(KV_CHUNK, S)
    return pl.pallas_call(
        functools.partial(_attn_kernel, k_slots=1, nchunks=S // tk, tk=tk),
        grid=(B, 4, S // tq),
        in_specs=[pl.BlockSpec((1, 128, tq), lambda b, h, i: (b, h, i)),
                  pl.BlockSpec((1, 1, S, 128), lambda b, h, i: (b, h, 0, 0)),
                  pl.BlockSpec((1, 64, S), lambda b, h, i: (b, h, 0))],
        out_specs=pl.BlockSpec((1, 64, tq), lambda b, h, i: (b, h, i)),
        out_shape=jax.ShapeDtypeStruct((B, 256, S), F32),
        compiler_params=pltpu.CompilerParams(vmem_limit_bytes=VMEM_LIMIT),
        name="attn_mla",
    )(qT, k, vT)


def _diff_kernel(slope_ref, lam_ref, q_ref, k_ref, v_ref, g_ref, o_ref, dist_ref,
                 *, nchunks, tk, out_scale, lambda_init):
    h = pl.program_id(1)
    i = pl.program_id(2)
    tq = q_ref.shape[2]
    slope = slope_ref[h]

    @pl.when(i == 0)
    def _():
        d = (lax.broadcasted_iota(jnp.int32, (tk, tq), 1)
             - lax.broadcasted_iota(jnp.int32, (tk, tq), 0)).astype(F32)
        dist_ref[...] = d * slope

    q32 = q_ref[0].astype(F32)
    slot0 = (h % 2) * 2
    qp = [_pad_rows(q32[32 * mp:32 * mp + 32], slot0 + mp, 4).astype(BF16) for mp in range(2)]
    q0 = i * tq

    def body(c, carry):
        off = pl.multiple_of(c * tk, tk)
        kc = k_ref[0, 0, pl.ds(off, tk), :]
        vc = v_ref[0, :, pl.ds(off, tk)]
        bias = -jnp.abs(dist_ref[...] + (q0 - off).astype(F32) * slope)
        out = []
        for mp in range(2):
            s = _dot(kc, qp[mp]) + bias
            out.append(_online_step(s, vc, carry[mp]))
        return tuple(out)

    init = _softmax_init(v_ref.shape[1], tq)
    (m1, l1, a1), (m2, l2, a2) = lax.fori_loop(0, nchunks, body, (init, init))
    lv = lam_ref[...]
    lam = (jnp.exp(jnp.sum(lv[0:1] * lv[1:2], axis=1, keepdims=True))
           - jnp.exp(jnp.sum(lv[2:3] * lv[3:4], axis=1, keepdims=True)) + lambda_init)
    o = a1 / l1 - lam * (a2 / l2)
    o_ref[0] = _rms_rows(o, g_ref[...]) * out_scale


def _attn_c(qT, k, vT, slopes2, lam_vecs, g_head, lambda_init):
    B, _, S = qT.shape
    tq, tk = min(Q_TILE, S), min(KV_CHUNK, S)
    smem = pl.BlockSpec(memory_space=pltpu.SMEM)
    return pl.pallas_call(
        functools.partial(_diff_kernel, nchunks=S // tk, tk=tk,
                          out_scale=1.0 - lambda_init, lambda_init=lambda_init),
        grid=(B, 4, S // tq),
        in_specs=[smem,
                  pl.BlockSpec((4, 32), lambda b, h, i: (0, 0)),
                  pl.BlockSpec((1, 64, tq), lambda b, h, i: (b, h, i)),
                  pl.BlockSpec((1, 1, S, 128), lambda b, h, i: (b, h // 2, 0, 0)),
                  pl.BlockSpec((1, 64, S), lambda b, h, i: (b, h, 0)),
                  pl.BlockSpec((64, 1), lambda b, h, i: (0, 0))],
        out_specs=pl.BlockSpec((1, 64, tq), lambda b, h, i: (b, h, i)),
        out_shape=jax.ShapeDtypeStruct((B, 256, S), F32),
        scratch_shapes=[pltpu.VMEM((tk, tq), F32)],
        compiler_params=pltpu.CompilerParams(vmem_limit_bytes=VMEM_LIMIT),
        name="attn_diff",
    )(slopes2, lam_vecs, qT, k, vT, g_head)


def _na_slab_row(a, n_rows):
    return jnp.clip(4 * a - 4, 0, n_rows - NA_SLAB_ROWS)


def _na_kernel(q_ref, k_ref, v_ref, bias_ref, o_ref, *, n_rows):
    h = pl.program_id(1)
    a = pl.program_id(2)
    nk = NA_SLAB_ROWS * GRID_W
    off = pl.multiple_of(_na_slab_row(a, n_rows) * GRID_W, 256)
    qp = _pad_rows(q_ref[0].astype(F32), h % 2, 2).astype(BF16)
    s = _dot(k_ref[0, 0, pl.ds(off, nk), :], qp) + bias_ref[0, 0]
    m = jnp.max(s, axis=0, keepdims=True)
    p = jnp.exp2(s - m)
    l = jnp.sum(p, axis=0, keepdims=True)
    o_ref[0] = _dot(v_ref[0, :, pl.ds(off, nk)], p.astype(BF16)) / l


def _na_bias_tables(rel_bias, n_rows):
    n_blocks = n_rows // 4
    kr_w = min(NA_ROWS, n_rows)
    n = np.arange(NA_SLAB_ROWS * GRID_W)
    qi = np.arange(NA_Q_TILE)

    def geometry(a):
        R0 = int(np.clip(4 * a - 4, 0, n_rows - NA_SLAB_ROWS))
        r = 4 * a + qi // GRID_W
        c = qi % GRID_W
        r0 = np.clip(r - kr_w // 2, 0, n_rows - kr_w)
        c0 = np.clip(c - NA_COLS // 2, 0, GRID_W - NA_COLS)
        kr = R0 + n // GRID_W
        kc = n % GRID_W
        valid = ((kr[:, None] >= r0[None, :]) & (kr[:, None] < r0[None, :] + kr_w)
                 & (kc[:, None] >= c0[None, :]) & (kc[:, None] < c0[None, :] + NA_COLS))
        dr = np.where(valid, kr[:, None] - r[None, :] + (NA_ROWS - 1), 0)
        dc = np.where(valid, kc[:, None] - c[None, :] + (NA_COLS - 1), 0)
        assert valid.sum(axis=0).min() == kr_w * NA_COLS
        return valid, dr, dc

    variants = [geometry(0), geometry(1), geometry(n_blocks - 1)]
    for a in range(1, n_blocks - 1):
        for u, w in zip(geometry(a), variants[1]):
            assert np.array_equal(u, w)
    tabs = []
    for valid, dr, dc in variants:
        g = rel_bias[:, dr, dc] * LOG2E
        tabs.append(jnp.where(valid[None], g, NEG_BIG))
    return jnp.stack(tabs, axis=1).astype(F32)


def _attn_d(qT, k, vT, bias_tab):
    B, _, S = qT.shape
    n_rows = S // GRID_W
    n_blocks = n_rows // 4
    nk = NA_SLAB_ROWS * GRID_W
    variant = lambda a: jnp.where(a == 0, 0, jnp.where(a == n_blocks - 1, 2, 1))
    return pl.pallas_call(
        functools.partial(_na_kernel, n_rows=n_rows),
        grid=(B, 4, n_blocks),
        in_specs=[pl.BlockSpec((1, 64, NA_Q_TILE), lambda b, h, a: (b, h, a)),
                  pl.BlockSpec((1, 1, S, 128), lambda b, h, a: (b, h // 2, 0, 0)),
                  pl.BlockSpec((1, 64, S), lambda b, h, a: (b, h, 0)),
                  pl.BlockSpec((1, 1, nk, NA_Q_TILE), lambda b, h, a: (h, variant(a), 0, 0))],
        out_specs=pl.BlockSpec((1, 64, NA_Q_TILE), lambda b, h, a: (b, h, a)),
        out_shape=jax.ShapeDtypeStruct((B, 256, S), F32),
        compiler_params=pltpu.CompilerParams(vmem_limit_bytes=VMEM_LIMIT),
        name="attn_nbr",
    )(qT, k, vT, bias_tab)


def _post_kernel(x_ref, oa_ref, ob_ref, oc_ref, od_ref, ga_ref, gb_ref, gd_ref,
                 wout_ref, gpost_ref, gmlp_ref, wup_ref, wdown_ref, gmlp_post_ref, y_ref,
                 *, ff_chunk):
    cat = jnp.concatenate([
        _rms_rows(oa_ref[0], ga_ref[...]).astype(BF16),
        _rms_rows(ob_ref[0], gb_ref[...]).astype(BF16),
        oc_ref[0].astype(BF16),
        _rms_rows(od_ref[0], gd_ref[...]).astype(BF16)], axis=0)
    mix = _dot(wout_ref[...], cat)
    x1 = x_ref[0] + _rms_rows(mix, gpost_ref[...])
    h = _rms_rows(x1, gmlp_ref[...]).astype(BF16)
    f = jnp.zeros_like(x1)
    d_ff = wup_ref.shape[0]
    for j in range(d_ff // ff_chunk):
        up = _dot(wup_ref[j * ff_chunk:(j + 1) * ff_chunk, :], h)
        act = jnp.square(jnp.maximum(up, 0.0)).astype(BF16)
        f = f + _dot(wdown_ref[:, j * ff_chunk:(j + 1) * ff_chunk], act)
    y_ref[0] = x1 + _rms_rows(f, gmlp_post_ref[...])


def _post(xT, oa, ob, oc, od, ga, gb, gd, woutT, gpost, gmlp, wupT, wdownT, gmlp_post):
    B, D, S = xT.shape
    tm = min(TOKEN_TILE, S)
    rows_spec = lambda rows: pl.BlockSpec((1, rows, tm), lambda b, i: (b, 0, i))
    consts = (ga, gb, gd, woutT, gpost, gmlp, wupT, wdownT, gmlp_post)
    return pl.pallas_call(
        functools.partial(_post_kernel, ff_chunk=1024),
        grid=(B, S // tm),
        in_specs=[rows_spec(D)] + [rows_spec(256)] * 4 + [_const_spec(c.shape) for c in consts],
        out_specs=rows_spec(D),
        out_shape=jax.ShapeDtypeStruct((B, D, S), F32),
        compiler_params=pltpu.CompilerParams(vmem_limit_bytes=VMEM_LIMIT),
        name="outproj_mlp",
    )(xT, oa, ob, oc, od, *consts)


def _rope_tables(n_tokens):
    def angles(pos, dim):
        inv = ROPE_BASE ** (-jnp.arange(0, dim, 2, dtype=F32) / dim)
        return pos.astype(F32)[:, None] * inv[None, :]

    t = jnp.arange(n_tokens, dtype=jnp.int32)
    out = []
    for ang in (angles(t // GRID_W, HEAD_DIM // 2), angles(t % GRID_W, HEAD_DIM // 2),
                angles(t, MLA_ROPE)):
        out += [jnp.cos(ang).T, jnp.sin(ang).T]
    return out


def kernel(x, norm_mix_pre, norm_mix_post, norm_mlp_pre, norm_mlp_post, w_in, a_q_norm, a_k_norm,
           b_cq_norm, b_ckv_norm, b_w_uq, b_w_ukv, c_lambda_q1, c_lambda_k1, c_lambda_q2,
           c_lambda_k2, d_rel_bias, g_out_a, g_out_b, g_out_c, g_out_d, w_out, w_up, w_down):
    B, S, D = x.shape
    depth = w_in.shape[0]
    col = lambda v: v.astype(F32).reshape(-1, 1)
    wt = lambda w: w.T.astype(BF16)
    tables = _rope_tables(S)
    slopes2 = jnp.exp2(-8.0 * jnp.arange(1, 5, dtype=F32) / 4) * LOG2E
    xT = jnp.transpose(x, (0, 2, 1))
    for l in range(depth):
        (qa, ka, va, qb, kb, vb, qc, kc, vc, qd, kd, vd) = _inproj(
            xT, col(norm_mix_pre[l]), wt(w_in[l]), col(a_q_norm[l]), col(a_k_norm[l]),
            col(b_cq_norm[l]), col(b_ckv_norm[l]), wt(b_w_uq[l]), wt(b_w_ukv[l]), tables)
        lambda_init = 0.8 - 0.6 * math.exp(-0.3 * l)
        lam_vecs = jnp.stack([c_lambda_q1[l], c_lambda_k1[l], c_lambda_q2[l], c_lambda_k2[l]]).astype(F32)
        oa = _attn_a(qa, ka, va)
        ob = _attn_b(qb, kb, vb)
        oc = _attn_c(qc, kc, vc, slopes2, lam_vecs, col(g_out_c[l]), lambda_init)
        od = _attn_d(qd, kd, vd, _na_bias_tables(d_rel_bias[l].astype(F32), S // GRID_W))
        xT = _post(xT, oa, ob, oc, od, col(g_out_a[l]), col(g_out_b[l]), col(g_out_d[l]),
                   wt(w_out[l]), col(norm_mix_post[l]), col(norm_mlp_pre[l]), wt(w_up[l]),
                   wt(w_down[l]), col(norm_mlp_post[l]))
    return jnp.transpose(xT, (0, 2, 1))
```

```python
import functools
import math

import numpy as np
import jax
import jax.numpy as jnp
from jax import lax
from jax.experimental import pallas as pl
from jax.experimental.pallas import tpu as pltpu

GRID_W = 64
HEAD_DIM = 64
GROUP_W = 256
N_KV_A = 2
MLA_Q_RANK = 192
MLA_KV_RANK = 128
MLA_NOPE = 64
MLA_ROPE = 32
DIFF_QK = 32
NA_ROWS = 8
NA_COLS = 16
ROPE_BASE = 10000.0
EPS = 1e-6
LOG2E = math.log2(math.e)
NEG_BIG = -1e30

A_COLS = 512
B_COLS = MLA_Q_RANK + MLA_KV_RANK + MLA_ROPE
C_COLS = 768
D_COLS = 768
OFF_B = A_COLS
OFF_C = OFF_B + B_COLS
OFF_D = OFF_C + C_COLS
IN_COLS = OFF_D + D_COLS

TOKEN_TILE = 512
Q_TILE = 512
KV_CHUNK = 256
KV_UNROLL = 16
NA_Q_TILE = 256
NA_SLAB_ROWS = 12
VMEM_LIMIT = 56 * 1024 * 1024

F32 = jnp.float32
BF16 = jnp.bfloat16


def _dot(a, b):
    return jnp.dot(a, b, preferred_element_type=F32)


def _rms_rows(x, gain):
    ms = jnp.mean(x * x, axis=0, keepdims=True)
    return x * lax.rsqrt(ms + EPS) * gain


def _rope_rows(x, cos, sin):
    n = cos.shape[0]
    x1, x2 = x[:n], x[n:]
    return x1 * cos - x2 * sin, x1 * sin + x2 * cos


def _inproj_kernel(x_ref, gpre_ref, w_ref, aq_ref, ak_ref, cqg_ref, ckvg_ref, wuq_ref, wukv_ref,
                   cr_ref, sr_ref, cc_ref, sc_ref, cs_ref, ss_ref,
                   qa_ref, ka_ref, va_ref, qb_ref, kb_ref, vb_ref,
                   qc_ref, kc_ref, vc_ref, qd_ref, kd_ref, vd_ref, kna_ref, knb_ref, knc_ref):
    x = x_ref[0]
    h = _rms_rows(x, gpre_ref[...]).astype(BF16)
    cr, sr, cc, sc = cr_ref[...], sr_ref[...], cc_ref[...], sc_ref[...]
    cs, ss = cs_ref[...], ss_ref[...]
    tm = x.shape[1]

    def axial(t):
        r1, r2 = _rope_rows(t[0:32], cr, sr)
        c1, c2 = _rope_rows(t[32:64], cc, sc)
        return r1, r2, c1, c2

    def sumsq(t):
        return jnp.sum(t * t, axis=0, keepdims=True)

    ua = _dot(w_ref[0:A_COLS, :], h)
    qscale = HEAD_DIM ** -0.5 * LOG2E
    for hd in range(4):
        t = _rms_rows(ua[64 * hd:64 * hd + 64], aq_ref[...])
        for j, piece in enumerate(axial(t)):
            qa_ref[0, 64 * hd + 16 * j:64 * hd + 16 * j + 16, :] = (piece * qscale).astype(BF16)
    kparts = []
    for g in range(N_KV_A):
        t = _rms_rows(ua[256 + 64 * g:256 + 64 * g + 64], ak_ref[...])
        kparts.extend(axial(t))
        kna_ref[0, g] = sumsq(t)
    ka_ref[0] = jnp.concatenate(kparts, axis=0).T.astype(BF16)
    va_ref[0] = ua[384:512].astype(BF16)

    ub = _dot(w_ref[OFF_B:OFF_B + B_COLS, :], h)
    cqn = _rms_rows(ub[0:MLA_Q_RANK], cqg_ref[...]).astype(BF16)
    ckvn = _rms_rows(ub[MLA_Q_RANK:MLA_Q_RANK + MLA_KV_RANK], ckvg_ref[...]).astype(BF16)
    kr1, kr2 = _rope_rows(ub[MLA_Q_RANK + MLA_KV_RANK:B_COLS], cs, ss)
    qb = _dot(wuq_ref[...], cqn)
    kvb = _dot(wukv_ref[...], ckvn)
    bscale = (MLA_NOPE + MLA_ROPE) ** -0.5 * LOG2E
    zeros32 = jnp.zeros((32, tm), F32)
    for hd in range(4):
        base = 96 * hd
        q1, q2 = _rope_rows(qb[base + 64:base + 96], cs, ss)
        qb_ref[0, 128 * hd:128 * hd + 64, :] = (qb[base:base + 64] * bscale).astype(BF16)
        qb_ref[0, 128 * hd + 64:128 * hd + 80, :] = (q1 * bscale).astype(BF16)
        qb_ref[0, 128 * hd + 80:128 * hd + 96, :] = (q2 * bscale).astype(BF16)
        qb_ref[0, 128 * hd + 96:128 * hd + 128, :] = jnp.zeros((32, tm), BF16)
        kfull = jnp.concatenate([kvb[128 * hd:128 * hd + 64], kr1, kr2, zeros32], axis=0)
        kb_ref[0, hd] = kfull.T.astype(BF16)
        knb_ref[0, hd] = sumsq(kfull)
        vb_ref[0, 64 * hd:64 * hd + 64, :] = kvb[128 * hd + 64:128 * hd + 128].astype(BF16)

    uc = _dot(w_ref[OFF_C:OFF_C + C_COLS, :], h)
    qc_ref[0] = (uc[0:256] * (DIFF_QK ** -0.5 * LOG2E)).astype(BF16)
    for p in range(2):
        kc_ref[0, p] = uc[256 + 128 * p:256 + 128 * p + 128].T.astype(BF16)
    vc_ref[0] = uc[512:768].astype(BF16)
    for hd in range(4):
        for mp in range(2):
            base = 256 + 64 * hd + 32 * mp
            knc_ref[0, hd, mp:mp + 1, :] = sumsq(uc[base:base + 32])

    ud = _dot(w_ref[OFF_D:OFF_D + D_COLS, :], h)
    qd_ref[0] = (ud[0:256] * (HEAD_DIM ** -0.5 * LOG2E)).astype(BF16)
    for p in range(2):
        kd_ref[0, p] = ud[256 + 128 * p:256 + 128 * p + 128].T.astype(BF16)
    vd_ref[0] = ud[512:768].astype(BF16)


def _const_spec(shape):
    nd = len(shape)
    return pl.BlockSpec(shape, lambda *_: (0,) * nd, pipeline_mode=pl.Buffered(1))


def _inproj(xT, gpre, wT, aq, ak, cqg, ckvg, wuqT, wukvT, tables):
    B, D, S = xT.shape
    tm = min(TOKEN_TILE, S)
    grid = (B, S // tm)
    tab_spec = pl.BlockSpec((16, tm), lambda b, i: (0, i))
    rows_spec = lambda rows: pl.BlockSpec((1, rows, tm), lambda b, i: (b, 0, i))
    out_shape = (
        jax.ShapeDtypeStruct((B, 256, S), BF16),
        jax.ShapeDtypeStruct((B, S, 128), BF16),
        jax.ShapeDtypeStruct((B, 128, S), BF16),
        jax.ShapeDtypeStruct((B, 512, S), BF16),
        jax.ShapeDtypeStruct((B, 4, S, 128), BF16),
        jax.ShapeDtypeStruct((B, 256, S), BF16),
        jax.ShapeDtypeStruct((B, 256, S), BF16),
        jax.ShapeDtypeStruct((B, 2, S, 128), BF16),
        jax.ShapeDtypeStruct((B, 256, S), BF16),
        jax.ShapeDtypeStruct((B, 256, S), BF16),
        jax.ShapeDtypeStruct((B, 2, S, 128), BF16),
        jax.ShapeDtypeStruct((B, 256, S), BF16),
        jax.ShapeDtypeStruct((B, 2, 1, S), F32),
        jax.ShapeDtypeStruct((B, 4, 1, S), F32),
        jax.ShapeDtypeStruct((B, 4, 2, S), F32),
    )
    kspec = lambda n: pl.BlockSpec((1, n, tm, 128), lambda b, i: (b, 0, i, 0))
    nspec = lambda n, r: pl.BlockSpec((1, n, r, tm), lambda b, i: (b, 0, 0, i))
    out_specs = (
        rows_spec(256), pl.BlockSpec((1, tm, 128), lambda b, i: (b, i, 0)), rows_spec(128),
        rows_spec(512), kspec(4), rows_spec(256),
        rows_spec(256), kspec(2), rows_spec(256),
        rows_spec(256), kspec(2), rows_spec(256),
        nspec(2, 1), nspec(4, 1), nspec(4, 2),
    )
    in_specs = [
        rows_spec(D), _const_spec(gpre.shape), _const_spec(wT.shape),
        _const_spec(aq.shape), _const_spec(ak.shape), _const_spec(cqg.shape), _const_spec(ckvg.shape),
        _const_spec(wuqT.shape), _const_spec(wukvT.shape),
    ] + [tab_spec] * 6
    return pl.pallas_call(
        _inproj_kernel, grid=grid, in_specs=in_specs, out_specs=out_specs, out_shape=out_shape,
        compiler_params=pltpu.CompilerParams(vmem_limit_bytes=VMEM_LIMIT),
        name="inproj",
    )(xT, gpre, wT, aq, ak, cqg, ckvg, wuqT, wukvT, *tables)


def _pad_rows(q32, slot, nslots):
    return jnp.concatenate([jnp.where(slot == j, q32, 0.0) for j in range(nslots)], axis=0)


def _softmax_stage(t, cst, m):
    m_new = jnp.maximum(m, jnp.max(t, axis=0, keepdims=True) + cst)
    alpha = jnp.exp2(m - m_new)
    p = jnp.exp2(t - (m_new - cst)).astype(BF16)
    return m_new, alpha, p


SHIFT_OK_MIN_SUM = 2.0 ** -64


def _attention(nchunks, tq, shifts, qk_stage, pv_stage, finalize):
    n_maps = len(shifts)
    dacc = pv_stage.rows
    acc0 = tuple(jnp.zeros((dacc, tq), F32) for _ in range(n_maps))

    unroll = math.gcd(nchunks, KV_UNROLL)

    def fixed_shift_trip(t, acc):
        c0 = t * unroll
        nxt = qk_stage(c0)
        for u in range(unroll):
            tiles, cst = nxt
            if u + 1 < unroll:
                nxt = qk_stage(c0 + u + 1)
            acc = tuple(acc[j] + pv_stage(c0 + u, jnp.exp2(tiles[j] - (shifts[j] - cst)).astype(BF16))
                        for j in range(n_maps))
        return acc

    acc = lax.fori_loop(0, nchunks // unroll, fixed_shift_trip, acc0)
    finalize(acc)
    lmin = functools.reduce(jnp.minimum, [jnp.min(a[dacc - 16:dacc - 15]) for a in acc])

    @pl.when(jnp.logical_not(lmin >= SHIFT_OK_MIN_SUM))
    def _():
        def online_trip(c, carry):
            m, acc = carry
            tiles, cst = qk_stage(c)
            out = []
            for j in range(n_maps):
                m_new, alpha, p = _softmax_stage(tiles[j], cst, m[j])
                out.append((m_new, alpha * acc[j] + pv_stage(c, p)))
            return tuple(o[0] for o in out), tuple(o[1] for o in out)

        m0 = tuple(jnp.full((1, tq), NEG_BIG, F32) for _ in range(n_maps))
        finalize(lax.fori_loop(0, nchunks, online_trip, (m0, acc0))[1])


def _make_pv_stage(v_ref, tk):
    dv = v_ref.shape[1]
    ones = jnp.ones((16, tk), BF16)

    def pv_stage(c, p):
        off = pl.multiple_of(c * tk, tk)
        vaug = jnp.concatenate([v_ref[0, :, pl.ds(off, tk)], ones], axis=0)
        return _dot(vaug, p)

    pv_stage.rows = dv + 16
    return pv_stage


def _score_bound(q32, kn2):
    return jnp.sqrt(jnp.sum(q32 * q32, axis=0, keepdims=True) * jnp.max(kn2, axis=1, keepdims=True))


def _attn_kernel(q_ref, k_ref, v_ref, kn_ref, o_ref, *, k_slots, nchunks, tk):
    q32 = q_ref[0].astype(F32)
    shift = _score_bound(q32, kn_ref[0, 0])
    if k_slots > 1:
        q32 = _pad_rows(q32, pl.program_id(1) // 2, k_slots)
    qp = q32.astype(BF16)
    tq = qp.shape[1]
    dv = v_ref.shape[1]
    k_at = (lambda off: k_ref[0, pl.ds(off, tk), :]) if len(k_ref.shape) == 3 else (
        lambda off: k_ref[0, 0, pl.ds(off, tk), :])

    def qk_stage(c):
        return (_dot(k_at(pl.multiple_of(c * tk, tk)), qp),), jnp.float32(0.0)

    def finalize(acc):
        o_ref[0] = acc[0][0:dv] / acc[0][dv:dv + 1]

    _attention(nchunks, tq, (shift,), qk_stage, _make_pv_stage(v_ref, tk), finalize)


def _attn_a(qT, k, vT, kn):
    B, _, S = qT.shape
    tq, tk = min(Q_TILE, S), min(KV_CHUNK, S)
    return pl.pallas_call(
        functools.partial(_attn_kernel, k_slots=2, nchunks=S // tk, tk=tk),
        grid=(B, 4, S // tq),
        in_specs=[pl.BlockSpec((1, 64, tq), lambda b, h, i: (b, h, i)),
                  pl.BlockSpec((1, S, 128), lambda b, h, i: (b, 0, 0)),
                  pl.BlockSpec((1, 64, S), lambda b, h, i: (b, h // 2, 0)),
                  pl.BlockSpec((1, 1, 1, S), lambda b, h, i: (b, h // 2, 0, 0))],
        out_specs=pl.BlockSpec((1, 64, tq), lambda b, h, i: (b, h, i)),
        out_shape=jax.ShapeDtypeStruct((B, 256, S), F32),
        compiler_params=pltpu.CompilerParams(vmem_limit_bytes=VMEM_LIMIT),
        name="attn_gqa",
    )(qT, k, vT, kn)


def _attn_b(qT, k, vT, kn):
    B, _, S = qT.shape
    tq, tk = min(Q_TILE, S), min(KV_CHUNK, S)
    return pl.pallas_call(
        functools.partial(_attn_kernel, k_slots=1, nchunks=S // tk, tk=tk),
        grid=(B, 4, S // tq),
        in_specs=[pl.BlockSpec((1, 128, tq), lambda b, h, i: (b, h, i)),
                  pl.BlockSpec((1, 1, S, 128), lambda b, h, i: (b, h, 0, 0)),
                  pl.BlockSpec((1, 64, S), lambda b, h, i: (b, h, 0)),
                  pl.BlockSpec((1, 1, 1, S), lambda b, h, i: (b, h, 0, 0))],
        out_specs=pl.BlockSpec((1, 64, tq), lambda b, h, i: (b, h, i)),
        out_shape=jax.ShapeDtypeStruct((B, 256, S), F32),
        compiler_params=pltpu.CompilerParams(vmem_limit_bytes=VMEM_LIMIT),
        name="attn_mla",
    )(qT, k, vT, kn)


def _diff_kernel(slope_ref, lam_ref, q_ref, k_ref, v_ref, kn_ref, g_ref, o_ref, bias_ref,
                 *, nchunks, tk, out_scale, lambda_init):
    h = pl.program_id(1)
    i = pl.program_id(2)
    tq = q_ref.shape[2]
    dv = v_ref.shape[1]
    n_inside = tq // tk
    slope = slope_ref[h]

    @pl.when(i == 0)
    def _():
        d = (lax.broadcasted_iota(jnp.int32, (tk, tq), 1)
             - lax.broadcasted_iota(jnp.int32, (tk, tq), 0)).astype(F32)
        bias_ref[0] = -d * slope
        bias_ref[1] = d * slope
        for j in range(n_inside):
            bias_ref[2 + j] = -jnp.abs(d - float(j * tk)) * slope

    q32 = q_ref[0].astype(F32)
    slot0 = (h % 2) * 2
    qp = [_pad_rows(q32[32 * mp:32 * mp + 32], slot0 + mp, 4).astype(BF16) for mp in range(2)]
    q0 = i * tq

    def qk_stage(c):
        off = pl.multiple_of(c * tk, tk)
        kc = k_ref[0, 0, pl.ds(off, tk), :]
        gap = q0 - off
        inside = jnp.logical_and(gap <= 0, gap > -tq)
        bias = bias_ref[jnp.where(gap > 0, 0, jnp.where(inside, 2 + (-gap) // tk, 1))]
        cst = jnp.where(inside, 0.0, -jnp.abs(gap).astype(F32) * slope)
        return tuple(_dot(kc, qp[mp]) + bias for mp in range(2)), cst

    def finalize(acc):
        a1, a2 = acc
        lv = lam_ref[...]
        lam = (jnp.exp(jnp.sum(lv[0:1] * lv[1:2], axis=1, keepdims=True))
               - jnp.exp(jnp.sum(lv[2:3] * lv[3:4], axis=1, keepdims=True)) + lambda_init)
        o = a1[0:dv] / a1[dv:dv + 1] - lam * (a2[0:dv] / a2[dv:dv + 1])
        o_ref[0] = _rms_rows(o, g_ref[...]) * out_scale

    shifts = tuple(_score_bound(q32[32 * mp:32 * mp + 32], kn_ref[0, 0, mp:mp + 1, :]) for mp in range(2))
    _attention(nchunks, tq, shifts, qk_stage, _make_pv_stage(v_ref, tk), finalize)


def _attn_c(qT, k, vT, kn, slopes2, lam_vecs, g_head, lambda_init):
    B, _, S = qT.shape
    tq, tk = min(Q_TILE, S), min(KV_CHUNK, S)
    smem = pl.BlockSpec(memory_space=pltpu.SMEM)
    return pl.pallas_call(
        functools.partial(_diff_kernel, nchunks=S // tk, tk=tk,
                          out_scale=1.0 - lambda_init, lambda_init=lambda_init),
        grid=(B, 4, S // tq),
        in_specs=[smem,
                  pl.BlockSpec((4, 32), lambda b, h, i: (0, 0)),
                  pl.BlockSpec((1, 64, tq), lambda b, h, i: (b, h, i)),
                  pl.BlockSpec((1, 1, S, 128), lambda b, h, i: (b, h // 2, 0, 0)),
                  pl.BlockSpec((1, 64, S), lambda b, h, i: (b, h, 0)),
                  pl.BlockSpec((1, 1, 2, S), lambda b, h, i: (b, h, 0, 0)),
                  pl.BlockSpec((64, 1), lambda b, h, i: (0, 0))],
        out_specs=pl.BlockSpec((1, 64, tq), lambda b, h, i: (b, h, i)),
        out_shape=jax.ShapeDtypeStruct((B, 256, S), F32),
        scratch_shapes=[pltpu.VMEM((2 + tq // tk, tk, tq), F32)],
        compiler_params=pltpu.CompilerParams(vmem_limit_bytes=VMEM_LIMIT),
        name="attn_diff",
    )(slopes2, lam_vecs, qT, k, vT, kn, g_head)


def _na_slab_row(a, n_rows):
    return jnp.clip(4 * a - 4, 0, n_rows - NA_SLAB_ROWS)


def _na_kernel(q_ref, k_ref, v_ref, bias_ref, o_ref, *, n_rows):
    h = pl.program_id(1)
    a = pl.program_id(2)
    nk = NA_SLAB_ROWS * GRID_W
    off = pl.multiple_of(_na_slab_row(a, n_rows) * GRID_W, 256)
    qp = _pad_rows(q_ref[0].astype(F32), h % 2, 2).astype(BF16)
    s = _dot(k_ref[0, 0, pl.ds(off, nk), :], qp) + bias_ref[0, 0]
    m = jnp.max(s, axis=0, keepdims=True)
    p = jnp.exp2(s - m).astype(BF16)
    dv = v_ref.shape[1]
    vaug = jnp.concatenate([v_ref[0, :, pl.ds(off, nk)], jnp.ones((16, nk), BF16)], axis=0)
    acc = _dot(vaug, p)
    o_ref[0] = acc[0:dv] / acc[dv:dv + 1]


def _na_bias_tables(rel_bias, n_rows):
    n_blocks = n_rows // 4
    kr_w = min(NA_ROWS, n_rows)
    col = np.arange(GRID_W)
    c0 = np.clip(col - NA_COLS // 2, 0, GRID_W - NA_COLS)
    col_ok = (col[:, None] >= c0[None, :]) & (col[:, None] < c0[None, :] + NA_COLS)
    dc = col[:, None] - col[None, :] + (NA_COLS - 1)
    onehot_c = ((dc[None] == np.arange(2 * NA_COLS - 1)[:, None, None]) & col_ok[None]).astype(np.float32)

    def row_geometry(a):
        R0 = int(np.clip(4 * a - 4, 0, n_rows - NA_SLAB_ROWS))
        r = 4 * a + np.arange(4)
        r0 = np.clip(r - kr_w // 2, 0, n_rows - kr_w)
        kr = R0 + np.arange(NA_SLAB_ROWS)
        ok = (kr[:, None] >= r0[None, :]) & (kr[:, None] < r0[None, :] + kr_w)
        dr = kr[:, None] - r[None, :] + (NA_ROWS - 1)
        return ((dr[:, :, None] == np.arange(2 * NA_ROWS - 1)) & ok[:, :, None]), ok

    variants = [row_geometry(0), row_geometry(1), row_geometry(n_blocks - 1)]
    for a in range(1, n_blocks - 1):
        assert np.array_equal(row_geometry(a)[0], variants[1][0])
    onehot_r = np.stack([v[0] for v in variants]).astype(np.float32)
    valid = (np.stack([v[1] for v in variants])[:, :, None, :, None]
             & col_ok[None, None, :, None, :])
    assert (valid.reshape(3, -1, NA_Q_TILE).sum(axis=1) == kr_w * NA_COLS).all()
    hi = lax.Precision.HIGHEST
    t1 = jnp.einsum('hrd,dkc->hrkc', rel_bias, onehot_c, precision=hi)
    tab = jnp.einsum('viqr,hrkc->hvikqc', onehot_r, t1, precision=hi) * LOG2E
    tab = jnp.where(valid[None], tab, NEG_BIG)
    return tab.reshape(4, 3, NA_SLAB_ROWS * GRID_W, NA_Q_TILE).astype(F32)


def _attn_d(qT, k, vT, bias_tab):
    B, _, S = qT.shape
    n_rows = S // GRID_W
    n_blocks = n_rows // 4
    nk = NA_SLAB_ROWS * GRID_W
    variant = lambda a: jnp.where(a == 0, 0, jnp.where(a == n_blocks - 1, 2, 1))
    return pl.pallas_call(
        functools.partial(_na_kernel, n_rows=n_rows),
        grid=(B, 4, n_blocks),
        in_specs=[pl.BlockSpec((1, 64, NA_Q_TILE), lambda b, h, a: (b, h, a)),
                  pl.BlockSpec((1, 1, S, 128), lambda b, h, a: (b, h // 2, 0, 0)),
                  pl.BlockSpec((1, 64, S), lambda b, h, a: (b, h, 0)),
                  pl.BlockSpec((1, 1, nk, NA_Q_TILE), lambda b, h, a: (h, variant(a), 0, 0))],
        out_specs=pl.BlockSpec((1, 64, NA_Q_TILE), lambda b, h, a: (b, h, a)),
        out_shape=jax.ShapeDtypeStruct((B, 256, S), F32),
        compiler_params=pltpu.CompilerParams(vmem_limit_bytes=VMEM_LIMIT),
        name="attn_nbr",
    )(qT, k, vT, bias_tab)


def _post_kernel(x_ref, oa_ref, ob_ref, oc_ref, od_ref, ga_ref, gb_ref, gd_ref,
                 wout_ref, gpost_ref, gmlp_ref, wup_ref, wdown_ref, gmlp_post_ref, y_ref,
                 *, ff_chunk):
    cat = jnp.concatenate([
        _rms_rows(oa_ref[0], ga_ref[...]).astype(BF16),
        _rms_rows(ob_ref[0], gb_ref[...]).astype(BF16),
        oc_ref[0].astype(BF16),
        _rms_rows(od_ref[0], gd_ref[...]).astype(BF16)], axis=0)
    mix = _dot(wout_ref[...], cat)
    x1 = x_ref[0] + _rms_rows(mix, gpost_ref[...])
    h = _rms_rows(x1, gmlp_ref[...]).astype(BF16)
    f = jnp.zeros_like(x1)
    d_ff = wup_ref.shape[0]
    for j in range(d_ff // ff_chunk):
        up = _dot(wup_ref[j * ff_chunk:(j + 1) * ff_chunk, :], h)
        act = jnp.square(jnp.maximum(up, 0.0)).astype(BF16)
        f = f + _dot(wdown_ref[:, j * ff_chunk:(j + 1) * ff_chunk], act)
    y_ref[0] = x1 + _rms_rows(f, gmlp_post_ref[...])


def _post(xT, oa, ob, oc, od, ga, gb, gd, woutT, gpost, gmlp, wupT, wdownT, gmlp_post):
    B, D, S = xT.shape
    tm = min(TOKEN_TILE, S)
    rows_spec = lambda rows: pl.BlockSpec((1, rows, tm), lambda b, i: (b, 0, i))
    consts = (ga, gb, gd, woutT, gpost, gmlp, wupT, wdownT, gmlp_post)
    return pl.pallas_call(
        functools.partial(_post_kernel, ff_chunk=1024),
        grid=(B, S // tm),
        in_specs=[rows_spec(D)] + [rows_spec(256)] * 4 + [_const_spec(c.shape) for c in consts],
        out_specs=rows_spec(D),
        out_shape=jax.ShapeDtypeStruct((B, D, S), F32),
        compiler_params=pltpu.CompilerParams(vmem_limit_bytes=VMEM_LIMIT),
        name="outproj_mlp",
    )(xT, oa, ob, oc, od, *consts)


def _rope_tables(n_tokens):
    def angles(pos, dim):
        inv = ROPE_BASE ** (-jnp.arange(0, dim, 2, dtype=F32) / dim)
        return pos.astype(F32)[:, None] * inv[None, :]

    t = jnp.arange(n_tokens, dtype=jnp.int32)
    out = []
    for ang in (angles(t // GRID_W, HEAD_DIM // 2), angles(t % GRID_W, HEAD_DIM // 2),
                angles(t, MLA_ROPE)):
        out += [jnp.cos(ang).T, jnp.sin(ang).T]
    return out


def kernel(x, norm_mix_pre, norm_mix_post, norm_mlp_pre, norm_mlp_post, w_in, a_q_norm, a_k_norm,
           b_cq_norm, b_ckv_norm, b_w_uq, b_w_ukv, c_lambda_q1, c_lambda_k1, c_lambda_q2,
           c_lambda_k2, d_rel_bias, g_out_a, g_out_b, g_out_c, g_out_d, w_out, w_up, w_down):
    B, S, D = x.shape
    depth = w_in.shape[0]
    col = lambda v: v.astype(F32).reshape(-1, 1)
    wt = lambda w: w.T.astype(BF16)
    tables = _rope_tables(S)
    slopes2 = jnp.exp2(-8.0 * jnp.arange(1, 5, dtype=F32) / 4) * LOG2E
    xT = jnp.transpose(x, (0, 2, 1))
    for l in range(depth):
        (qa, ka, va, qb, kb, vb, qc, kc, vc, qd, kd, vd, kna, knb, knc) = _inproj(
            xT, col(norm_mix_pre[l]), wt(w_in[l]), col(a_q_norm[l]), col(a_k_norm[l]),
            col(b_cq_norm[l]), col(b_ckv_norm[l]), wt(b_w_uq[l]), wt(b_w_ukv[l]), tables)
        lambda_init = 0.8 - 0.6 * math.exp(-0.3 * l)
        lam_vecs = jnp.stack([c_lambda_q1[l], c_lambda_k1[l], c_lambda_q2[l], c_lambda_k2[l]]).astype(F32)
        oa = _attn_a(qa, ka, va, kna)
        ob = _attn_b(qb, kb, vb, knb)
        oc = _attn_c(qc, kc, vc, knc, slopes2, lam_vecs, col(g_out_c[l]), lambda_init)
        od = _attn_d(qd, kd, vd, _na_bias_tables(d_rel_bias[l].astype(F32), S // GRID_W))
        xT = _post(xT, oa, ob, oc, od, col(g_out_a[l]), col(g_out_b[l]), col(g_out_d[l]),
                   wt(w_out[l]), col(norm_mix_post[l]), col(norm_mlp_pre[l]), wt(w_up[l]),
                   wt(w_down[l]), col(norm_mlp_post[l]))
    return jnp.transpose(xT, (0, 2, 1))
```

```python
import functools
import math

import numpy as np
import jax
import jax.numpy as jnp
from jax import lax
from jax.experimental import pallas as pl
from jax.experimental.pallas import tpu as pltpu

GRID_W = 64
HEAD_DIM = 64
GROUP_W = 256
N_KV_A = 2
MLA_Q_RANK = 192
MLA_KV_RANK = 128
MLA_NOPE = 64
MLA_ROPE = 32
DIFF_QK = 32
NA_ROWS = 8
NA_COLS = 16
ROPE_BASE = 10000.0
EPS = 1e-6
LOG2E = math.log2(math.e)
NEG_BIG = -1e30

A_COLS = 512
B_COLS = MLA_Q_RANK + MLA_KV_RANK + MLA_ROPE
C_COLS = 768
D_COLS = 768
OFF_B = A_COLS
OFF_C = OFF_B + B_COLS
OFF_D = OFF_C + C_COLS
IN_COLS = OFF_D + D_COLS

TOKEN_TILE = 512
Q_TILE = 512
KV_CHUNK = 256
QK_LOOKAHEAD = 2
KV_UNROLL = 16
NA_Q_TILE = 256
NA_SLAB_ROWS = 12
VMEM_LIMIT = 56 * 1024 * 1024

F32 = jnp.float32
BF16 = jnp.bfloat16


def _dot(a, b):
    return jnp.dot(a, b, preferred_element_type=F32)


def _rms_rows(x, gain):
    ms = jnp.mean(x * x, axis=0, keepdims=True)
    return x * lax.rsqrt(ms + EPS) * gain


def _rope_rows(x, cos, sin):
    n = cos.shape[0]
    x1, x2 = x[:n], x[n:]
    return x1 * cos - x2 * sin, x1 * sin + x2 * cos


def _inproj_kernel(x_ref, gpre_ref, w_ref, aq_ref, ak_ref, cqg_ref, ckvg_ref, wuq_ref, wukv_ref,
                   cr_ref, sr_ref, cc_ref, sc_ref, cs_ref, ss_ref,
                   qa_ref, ka_ref, va_ref, qb_ref, kb_ref, vb_ref,
                   qc_ref, kc_ref, vc_ref, qd_ref, kd_ref, vd_ref,
                   kna_ref, knb_ref, knc_ref, knd_ref):
    x = x_ref[0]
    h = _rms_rows(x, gpre_ref[...]).astype(BF16)
    cr, sr, cc, sc = cr_ref[...], sr_ref[...], cc_ref[...], sc_ref[...]
    cs, ss = cs_ref[...], ss_ref[...]
    tm = x.shape[1]

    def axial(t):
        r1, r2 = _rope_rows(t[0:32], cr, sr)
        c1, c2 = _rope_rows(t[32:64], cc, sc)
        return r1, r2, c1, c2

    def sumsq(t):
        return jnp.sum(t * t, axis=0, keepdims=True)

    ua = _dot(w_ref[0:A_COLS, :], h)
    qscale = HEAD_DIM ** -0.5 * LOG2E
    for hd in range(4):
        t = _rms_rows(ua[64 * hd:64 * hd + 64], aq_ref[...])
        for j, piece in enumerate(axial(t)):
            qa_ref[0, 64 * hd + 16 * j:64 * hd + 16 * j + 16, :] = (piece * qscale).astype(BF16)
    kparts = []
    for g in range(N_KV_A):
        t = _rms_rows(ua[256 + 64 * g:256 + 64 * g + 64], ak_ref[...])
        kparts.extend(axial(t))
        kna_ref[0, g] = sumsq(t)
    ka_ref[0] = jnp.concatenate(kparts, axis=0).T.astype(BF16)
    va_ref[0] = ua[384:512].astype(BF16)

    ub = _dot(w_ref[OFF_B:OFF_B + B_COLS, :], h)
    cqn = _rms_rows(ub[0:MLA_Q_RANK], cqg_ref[...]).astype(BF16)
    ckvn = _rms_rows(ub[MLA_Q_RANK:MLA_Q_RANK + MLA_KV_RANK], ckvg_ref[...]).astype(BF16)
    kr1, kr2 = _rope_rows(ub[MLA_Q_RANK + MLA_KV_RANK:B_COLS], cs, ss)
    qb = _dot(wuq_ref[...], cqn)
    kvb = _dot(wukv_ref[...], ckvn)
    bscale = (MLA_NOPE + MLA_ROPE) ** -0.5 * LOG2E
    zeros32 = jnp.zeros((32, tm), F32)
    for hd in range(4):
        base = 96 * hd
        q1, q2 = _rope_rows(qb[base + 64:base + 96], cs, ss)
        qb_ref[0, 128 * hd:128 * hd + 64, :] = (qb[base:base + 64] * bscale).astype(BF16)
        qb_ref[0, 128 * hd + 64:128 * hd + 80, :] = (q1 * bscale).astype(BF16)
        qb_ref[0, 128 * hd + 80:128 * hd + 96, :] = (q2 * bscale).astype(BF16)
        qb_ref[0, 128 * hd + 96:128 * hd + 128, :] = jnp.zeros((32, tm), BF16)
        kfull = jnp.concatenate([kvb[128 * hd:128 * hd + 64], kr1, kr2, zeros32], axis=0)
        kb_ref[0, hd] = kfull.T.astype(BF16)
        knb_ref[0, hd] = sumsq(kfull)
        vb_ref[0, 64 * hd:64 * hd + 64, :] = kvb[128 * hd + 64:128 * hd + 128].astype(BF16)

    uc = _dot(w_ref[OFF_C:OFF_C + C_COLS, :], h)
    qc_ref[0] = (uc[0:256] * (DIFF_QK ** -0.5 * LOG2E)).astype(BF16)
    for p in range(2):
        kc_ref[0, p] = uc[256 + 128 * p:256 + 128 * p + 128].T.astype(BF16)
    vc_ref[0] = uc[512:768].astype(BF16)
    for hd in range(4):
        for mp in range(2):
            base = 256 + 64 * hd + 32 * mp
            knc_ref[0, hd, mp:mp + 1, :] = sumsq(uc[base:base + 32])

    ud = _dot(w_ref[OFF_D:OFF_D + D_COLS, :], h)
    qd_ref[0] = (ud[0:256] * (HEAD_DIM ** -0.5 * LOG2E)).astype(BF16)
    for p in range(2):
        kd_ref[0, p] = ud[256 + 128 * p:256 + 128 * p + 128].T.astype(BF16)
    vd_ref[0] = ud[512:768].astype(BF16)
    for hd in range(4):
        knd_ref[0, hd] = sumsq(ud[256 + 64 * hd:256 + 64 * hd + 64])


def _const_spec(shape):
    nd = len(shape)
    return pl.BlockSpec(shape, lambda *_: (0,) * nd, pipeline_mode=pl.Buffered(1))


def _inproj(xT, gpre, wT, aq, ak, cqg, ckvg, wuqT, wukvT, tables):
    B, D, S = xT.shape
    tm = min(TOKEN_TILE, S)
    grid = (B, S // tm)
    tab_spec = pl.BlockSpec((16, tm), lambda b, i: (0, i))
    rows_spec = lambda rows: pl.BlockSpec((1, rows, tm), lambda b, i: (b, 0, i))
    out_shape = (
        jax.ShapeDtypeStruct((B, 256, S), BF16),
        jax.ShapeDtypeStruct((B, S, 128), BF16),
        jax.ShapeDtypeStruct((B, 128, S), BF16),
        jax.ShapeDtypeStruct((B, 512, S), BF16),
        jax.ShapeDtypeStruct((B, 4, S, 128), BF16),
        jax.ShapeDtypeStruct((B, 256, S), BF16),
        jax.ShapeDtypeStruct((B, 256, S), BF16),
        jax.ShapeDtypeStruct((B, 2, S, 128), BF16),
        jax.ShapeDtypeStruct((B, 256, S), BF16),
        jax.ShapeDtypeStruct((B, 256, S), BF16),
        jax.ShapeDtypeStruct((B, 2, S, 128), BF16),
        jax.ShapeDtypeStruct((B, 256, S), BF16),
        jax.ShapeDtypeStruct((B, 2, 1, S), F32),
        jax.ShapeDtypeStruct((B, 4, 1, S), F32),
        jax.ShapeDtypeStruct((B, 4, 2, S), F32),
        jax.ShapeDtypeStruct((B, 4, 1, S), F32),
    )
    kspec = lambda n: pl.BlockSpec((1, n, tm, 128), lambda b, i: (b, 0, i, 0))
    nspec = lambda n, r: pl.BlockSpec((1, n, r, tm), lambda b, i: (b, 0, 0, i))
    out_specs = (
        rows_spec(256), pl.BlockSpec((1, tm, 128), lambda b, i: (b, i, 0)), rows_spec(128),
        rows_spec(512), kspec(4), rows_spec(256),
        rows_spec(256), kspec(2), rows_spec(256),
        rows_spec(256), kspec(2), rows_spec(256),
        nspec(2, 1), nspec(4, 1), nspec(4, 2), nspec(4, 1),
    )
    in_specs = [
        rows_spec(D), _const_spec(gpre.shape), _const_spec(wT.shape),
        _const_spec(aq.shape), _const_spec(ak.shape), _const_spec(cqg.shape), _const_spec(ckvg.shape),
        _const_spec(wuqT.shape), _const_spec(wukvT.shape),
    ] + [tab_spec] * 6
    return pl.pallas_call(
        _inproj_kernel, grid=grid, in_specs=in_specs, out_specs=out_specs, out_shape=out_shape,
        compiler_params=pltpu.CompilerParams(vmem_limit_bytes=VMEM_LIMIT),
        name="inproj",
    )(xT, gpre, wT, aq, ak, cqg, ckvg, wuqT, wukvT, *tables)


def _pad_rows(q32, slot, nslots):
    return jnp.concatenate([jnp.where(slot == j, q32, 0.0) for j in range(nslots)], axis=0)


def _softmax_stage(t, cst, m):
    m_new = jnp.maximum(m, jnp.max(t, axis=0, keepdims=True) + cst)
    alpha = jnp.exp2(m - m_new)
    p = jnp.exp2(t - (m_new - cst)).astype(BF16)
    return m_new, alpha, p


SHIFT_OK_MIN_SUM = 2.0 ** -64


def _attention(nchunks, tq, shifts, qk_stage, pv_stage, finalize):
    n_maps = len(shifts)
    dacc = pv_stage.rows
    acc0 = tuple(jnp.zeros((dacc, tq), F32) for _ in range(n_maps))

    unroll = math.gcd(nchunks, KV_UNROLL)
    tasks = [(u, j) for u in range(unroll) for j in range(n_maps)]

    def fixed_shift_trip(t, acc):
        c0 = t * unroll
        acc = list(acc)
        queue = []
        for idx, (u, j) in enumerate(tasks):
            while len(queue) + idx < min(idx + QK_LOOKAHEAD + 1, len(tasks)):
                uu, jj = tasks[idx + len(queue)]
                queue.append(qk_stage(c0 + uu, jj))
            tile, cst = queue.pop(0)
            acc[j] = acc[j] + pv_stage(c0 + u, jnp.exp2(tile - (shifts[j] - cst)).astype(BF16))
        return tuple(acc)

    acc = lax.fori_loop(0, nchunks // unroll, fixed_shift_trip, acc0)
    finalize(acc)
    lmin = functools.reduce(jnp.minimum, [jnp.min(a[dacc - 16:dacc - 15]) for a in acc])

    @pl.when(jnp.logical_not(lmin >= SHIFT_OK_MIN_SUM))
    def _():
        def online_trip(c, carry):
            m, acc = carry
            out = []
            for j in range(n_maps):
                tile, cst = qk_stage(c, j)
                m_new, alpha, p = _softmax_stage(tile, cst, m[j])
                out.append((m_new, alpha * acc[j] + pv_stage(c, p)))
            return tuple(o[0] for o in out), tuple(o[1] for o in out)

        m0 = tuple(jnp.full((1, tq), NEG_BIG, F32) for _ in range(n_maps))
        finalize(lax.fori_loop(0, nchunks, online_trip, (m0, acc0))[1])


def _make_pv_stage(v_ref, tk):
    dv = v_ref.shape[1]
    ones = jnp.ones((16, tk), BF16)

    def pv_stage(c, p):
        off = pl.multiple_of(c * tk, tk)
        vaug = jnp.concatenate([v_ref[0, :, pl.ds(off, tk)], ones], axis=0)
        return _dot(vaug, p)

    pv_stage.rows = dv + 16
    return pv_stage


def _score_bound(q32, kn2):
    return jnp.sqrt(jnp.sum(q32 * q32, axis=0, keepdims=True) * jnp.max(kn2, axis=1, keepdims=True))


def _attn_kernel(q_ref, k_ref, v_ref, kn_ref, o_ref, *, k_slots, nchunks, tk):
    q32 = q_ref[0].astype(F32)
    shift = _score_bound(q32, kn_ref[0, 0])
    if k_slots > 1:
        q32 = _pad_rows(q32, pl.program_id(1) // 2, k_slots)
    qp = q32.astype(BF16)
    tq = qp.shape[1]
    dv = v_ref.shape[1]
    k_at = (lambda off: k_ref[0, pl.ds(off, tk), :]) if len(k_ref.shape) == 3 else (
        lambda off: k_ref[0, 0, pl.ds(off, tk), :])

    def qk_stage(c, j):
        return _dot(k_at(pl.multiple_of(c * tk, tk)), qp), jnp.float32(0.0)

    def finalize(acc):
        o_ref[0] = acc[0][0:dv] / acc[0][dv:dv + 1]

    _attention(nchunks, tq, (shift,), qk_stage, _make_pv_stage(v_ref, tk), finalize)


def _attn_a(qT, k, vT, kn):
    B, _, S = qT.shape
    tq, tk = min(Q_TILE, S), min(KV_CHUNK, S)
    return pl.pallas_call(
        functools.partial(_attn_kernel, k_slots=2, nchunks=S // tk, tk=tk),
        grid=(B, 4, S // tq),
        in_specs=[pl.BlockSpec((1, 64, tq), lambda b, h, i: (b, h, i)),
                  pl.BlockSpec((1, S, 128), lambda b, h, i: (b, 0, 0)),
                  pl.BlockSpec((1, 64, S), lambda b, h, i: (b, h // 2, 0)),
                  pl.BlockSpec((1, 1, 1, S), lambda b, h, i: (b, h // 2, 0, 0))],
        out_specs=pl.BlockSpec((1, 64, tq), lambda b, h, i: (b, h, i)),
        out_shape=jax.ShapeDtypeStruct((B, 256, S), F32),
        compiler_params=pltpu.CompilerParams(vmem_limit_bytes=VMEM_LIMIT),
        name="attn_gqa",
    )(qT, k, vT, kn)


def _attn_b(qT, k, vT, kn):
    B, _, S = qT.shape
    tq, tk = min(Q_TILE, S), min(KV_CHUNK, S)
    return pl.pallas_call(
        functools.partial(_attn_kernel, k_slots=1, nchunks=S // tk, tk=tk),
        grid=(B, 4, S // tq),
        in_specs=[pl.BlockSpec((1, 128, tq), lambda b, h, i: (b, h, i)),
                  pl.BlockSpec((1, 1, S, 128), lambda b, h, i: (b, h, 0, 0)),
                  pl.BlockSpec((1, 64, S), lambda b, h, i: (b, h, 0)),
                  pl.BlockSpec((1, 1, 1, S), lambda b, h, i: (b, h, 0, 0))],
        out_specs=pl.BlockSpec((1, 64, tq), lambda b, h, i: (b, h, i)),
        out_shape=jax.ShapeDtypeStruct((B, 256, S), F32),
        compiler_params=pltpu.CompilerParams(vmem_limit_bytes=VMEM_LIMIT),
        name="attn_mla",
    )(qT, k, vT, kn)


def _diff_kernel(slope_ref, lam_ref, q_ref, k_ref, v_ref, kn_ref, g_ref, o_ref, bias_ref,
                 *, nchunks, tk, out_scale, lambda_init):
    h = pl.program_id(1)
    i = pl.program_id(2)
    tq = q_ref.shape[2]
    dv = v_ref.shape[1]
    n_inside = tq // tk
    slope = slope_ref[h]

    @pl.when(i == 0)
    def _():
        d = (lax.broadcasted_iota(jnp.int32, (tk, tq), 1)
             - lax.broadcasted_iota(jnp.int32, (tk, tq), 0)).astype(F32)
        bias_ref[0] = -d * slope
        bias_ref[1] = d * slope
        for j in range(n_inside):
            bias_ref[2 + j] = -jnp.abs(d - float(j * tk)) * slope

    q32 = q_ref[0].astype(F32)
    slot0 = (h % 2) * 2
    qp = [_pad_rows(q32[32 * mp:32 * mp + 32], slot0 + mp, 4).astype(BF16) for mp in range(2)]
    q0 = i * tq

    def qk_stage(c, mp):
        off = pl.multiple_of(c * tk, tk)
        kc = k_ref[0, 0, pl.ds(off, tk), :]
        gap = q0 - off
        inside = jnp.logical_and(gap <= 0, gap > -tq)
        bias = bias_ref[jnp.where(gap > 0, 0, jnp.where(inside, 2 + (-gap) // tk, 1))]
        cst = jnp.where(inside, 0.0, -jnp.abs(gap).astype(F32) * slope)
        return _dot(kc, qp[mp]) + bias, cst

    def finalize(acc):
        a1, a2 = acc
        lv = lam_ref[...]
        lam = (jnp.exp(jnp.sum(lv[0:1] * lv[1:2], axis=1, keepdims=True))
               - jnp.exp(jnp.sum(lv[2:3] * lv[3:4], axis=1, keepdims=True)) + lambda_init)
        o = a1[0:dv] / a1[dv:dv + 1] - lam * (a2[0:dv] / a2[dv:dv + 1])
        o_ref[0] = _rms_rows(o, g_ref[...]) * out_scale

    shifts = tuple(_score_bound(q32[32 * mp:32 * mp + 32], kn_ref[0, 0, mp:mp + 1, :]) for mp in range(2))
    _attention(nchunks, tq, shifts, qk_stage, _make_pv_stage(v_ref, tk), finalize)


def _attn_c(qT, k, vT, kn, slopes2, lam_vecs, g_head, lambda_init):
    B, _, S = qT.shape
    tq, tk = min(Q_TILE, S), min(KV_CHUNK, S)
    smem = pl.BlockSpec(memory_space=pltpu.SMEM)
    return pl.pallas_call(
        functools.partial(_diff_kernel, nchunks=S // tk, tk=tk,
                          out_scale=1.0 - lambda_init, lambda_init=lambda_init),
        grid=(B, 4, S // tq),
        in_specs=[smem,
                  pl.BlockSpec((4, 32), lambda b, h, i: (0, 0)),
                  pl.BlockSpec((1, 64, tq), lambda b, h, i: (b, h, i)),
                  pl.BlockSpec((1, 1, S, 128), lambda b, h, i: (b, h // 2, 0, 0)),
                  pl.BlockSpec((1, 64, S), lambda b, h, i: (b, h, 0)),
                  pl.BlockSpec((1, 1, 2, S), lambda b, h, i: (b, h, 0, 0)),
                  pl.BlockSpec((64, 1), lambda b, h, i: (0, 0))],
        out_specs=pl.BlockSpec((1, 64, tq), lambda b, h, i: (b, h, i)),
        out_shape=jax.ShapeDtypeStruct((B, 256, S), F32),
        scratch_shapes=[pltpu.VMEM((2 + tq // tk, tk, tq), F32)],
        compiler_params=pltpu.CompilerParams(vmem_limit_bytes=VMEM_LIMIT),
        name="attn_diff",
    )(slopes2, lam_vecs, qT, k, vT, kn, g_head)


def _na_kernel(bmax_ref, q_ref, k_ref, v_ref, kn_ref, bias_ref, o_ref, *, n_rows):
    h = pl.program_id(0)
    n_blocks = n_rows // 4
    nk = NA_SLAB_ROWS * GRID_W
    tq = NA_Q_TILE
    dv = v_ref.shape[1]
    ones = jnp.ones((16, nk), BF16)
    kmax2 = jnp.max(kn_ref[0, 0], axis=1, keepdims=True)
    bmax = bmax_ref[h]

    def variant(a):
        return 0 if a == 0 else (2 if a == n_blocks - 1 else 1)

    def slab(a):
        return min(max(4 * a - 4, 0), n_rows - NA_SLAB_ROWS) * GRID_W

    def q_block(a):
        return q_ref[0, :, a * tq:(a + 1) * tq].astype(F32)

    def scores(a):
        qp = _pad_rows(q_block(a), h % 2, 2).astype(BF16)
        return _dot(k_ref[0, 0, slab(a):slab(a) + nk, :], qp) + bias_ref[0, variant(a)]

    def pv(a, p):
        vaug = jnp.concatenate([v_ref[0, :, slab(a):slab(a) + nk], ones], axis=0)
        return _dot(vaug, p)

    lmin = None
    nxt = scores(0)
    for a in range(n_blocks):
        s = nxt
        if a + 1 < n_blocks:
            nxt = scores(a + 1)
        q32 = q_block(a)
        shift = jnp.sqrt(jnp.sum(q32 * q32, axis=0, keepdims=True) * kmax2) + bmax
        acc = pv(a, jnp.exp2(s - shift).astype(BF16))
        o_ref[0, :, a * tq:(a + 1) * tq] = acc[0:dv] / acc[dv:dv + 1]
        amin = jnp.min(acc[dv:dv + 1])
        lmin = amin if lmin is None else jnp.minimum(lmin, amin)

    @pl.when(jnp.logical_not(lmin >= SHIFT_OK_MIN_SUM))
    def _():
        for a in range(n_blocks):
            s = scores(a)
            p = jnp.exp2(s - jnp.max(s, axis=0, keepdims=True)).astype(BF16)
            acc = pv(a, p)
            o_ref[0, :, a * tq:(a + 1) * tq] = acc[0:dv] / acc[dv:dv + 1]


def _na_bias_tables(rel_bias, n_rows):
    n_blocks = n_rows // 4
    kr_w = min(NA_ROWS, n_rows)
    col = np.arange(GRID_W)
    c0 = np.clip(col - NA_COLS // 2, 0, GRID_W - NA_COLS)
    col_ok = (col[:, None] >= c0[None, :]) & (col[:, None] < c0[None, :] + NA_COLS)
    dc = col[:, None] - col[None, :] + (NA_COLS - 1)
    onehot_c = ((dc[None] == np.arange(2 * NA_COLS - 1)[:, None, None]) & col_ok[None]).astype(np.float32)

    def row_geometry(a):
        R0 = int(np.clip(4 * a - 4, 0, n_rows - NA_SLAB_ROWS))
        r = 4 * a + np.arange(4)
        r0 = np.clip(r - kr_w // 2, 0, n_rows - kr_w)
        kr = R0 + np.arange(NA_SLAB_ROWS)
        ok = (kr[:, None] >= r0[None, :]) & (kr[:, None] < r0[None, :] + kr_w)
        dr = kr[:, None] - r[None, :] + (NA_ROWS - 1)
        return ((dr[:, :, None] == np.arange(2 * NA_ROWS - 1)) & ok[:, :, None]), ok

    variants = [row_geometry(0), row_geometry(1), row_geometry(n_blocks - 1)]
    for a in range(1, n_blocks - 1):
        assert np.array_equal(row_geometry(a)[0], variants[1][0])
    onehot_r = np.stack([v[0] for v in variants]).astype(np.float32)
    valid = (np.stack([v[1] for v in variants])[:, :, None, :, None]
             & col_ok[None, None, :, None, :])
    assert (valid.reshape(3, -1, NA_Q_TILE).sum(axis=1) == kr_w * NA_COLS).all()
    hi = lax.Precision.HIGHEST
    t1 = jnp.einsum('hrd,dkc->hrkc', rel_bias, onehot_c, precision=hi)
    tab = jnp.einsum('viqr,hrkc->hvikqc', onehot_r, t1, precision=hi) * LOG2E
    tab = jnp.where(valid[None], tab, NEG_BIG)
    return tab.reshape(4, 3, NA_SLAB_ROWS * GRID_W, NA_Q_TILE).astype(F32)


def _attn_d(qT, k, vT, kn, bias_tab, bias_max):
    B, _, S = qT.shape
    n_rows = S // GRID_W
    nk = NA_SLAB_ROWS * GRID_W
    return pl.pallas_call(
        functools.partial(_na_kernel, n_rows=n_rows),
        grid=(4, B),
        in_specs=[pl.BlockSpec(memory_space=pltpu.SMEM),
                  pl.BlockSpec((1, 64, S), lambda h, b: (b, h, 0)),
                  pl.BlockSpec((1, 1, S, 128), lambda h, b: (b, h // 2, 0, 0)),
                  pl.BlockSpec((1, 64, S), lambda h, b: (b, h, 0)),
                  pl.BlockSpec((1, 1, 1, S), lambda h, b: (b, h, 0, 0)),
                  pl.BlockSpec((1, 3, nk, NA_Q_TILE), lambda h, b: (h, 0, 0, 0))],
        out_specs=pl.BlockSpec((1, 64, S), lambda h, b: (b, h, 0)),
        out_shape=jax.ShapeDtypeStruct((B, 256, S), F32),
        compiler_params=pltpu.CompilerParams(vmem_limit_bytes=VMEM_LIMIT),
        name="attn_nbr",
    )(bias_max, qT, k, vT, kn, bias_tab)


def _post_kernel(x_ref, oa_ref, ob_ref, oc_ref, od_ref, ga_ref, gb_ref, gd_ref,
                 wout_ref, gpost_ref, gmlp_ref, wup_ref, wdown_ref, gmlp_post_ref, y_ref,
                 *, ff_chunk):
    cat = jnp.concatenate([
        _rms_rows(oa_ref[0], ga_ref[...]).astype(BF16),
        _rms_rows(ob_ref[0], gb_ref[...]).astype(BF16),
        oc_ref[0].astype(BF16),
        _rms_rows(od_ref[0], gd_ref[...]).astype(BF16)], axis=0)
    mix = _dot(wout_ref[...], cat)
    x1 = x_ref[0] + _rms_rows(mix, gpost_ref[...])
    h = _rms_rows(x1, gmlp_ref[...]).astype(BF16)
    f = jnp.zeros_like(x1)
    d_ff = wup_ref.shape[0]
    for j in range(d_ff // ff_chunk):
        up = _dot(wup_ref[j * ff_chunk:(j + 1) * ff_chunk, :], h)
        act = jnp.square(jnp.maximum(up, 0.0)).astype(BF16)
        f = f + _dot(wdown_ref[:, j * ff_chunk:(j + 1) * ff_chunk], act)
    y_ref[0] = x1 + _rms_rows(f, gmlp_post_ref[...])


def _post(xT, oa, ob, oc, od, ga, gb, gd, woutT, gpost, gmlp, wupT, wdownT, gmlp_post):
    B, D, S = xT.shape
    tm = min(TOKEN_TILE, S)
    rows_spec = lambda rows: pl.BlockSpec((1, rows, tm), lambda b, i: (b, 0, i))
    consts = (ga, gb, gd, woutT, gpost, gmlp, wupT, wdownT, gmlp_post)
    return pl.pallas_call(
        functools.partial(_post_kernel, ff_chunk=1024),
        grid=(B, S // tm),
        in_specs=[rows_spec(D)] + [rows_spec(256)] * 4 + [_const_spec(c.shape) for c in consts],
        out_specs=rows_spec(D),
        out_shape=jax.ShapeDtypeStruct((B, D, S), F32),
        compiler_params=pltpu.CompilerParams(vmem_limit_bytes=VMEM_LIMIT),
        name="outproj_mlp",
    )(xT, oa, ob, oc, od, *consts)


def _rope_tables(n_tokens):
    def angles(pos, dim):
        inv = ROPE_BASE ** (-jnp.arange(0, dim, 2, dtype=F32) / dim)
        return pos.astype(F32)[:, None] * inv[None, :]

    t = jnp.arange(n_tokens, dtype=jnp.int32)
    out = []
    for ang in (angles(t // GRID_W, HEAD_DIM // 2), angles(t % GRID_W, HEAD_DIM // 2),
                angles(t, MLA_ROPE)):
        out += [jnp.cos(ang).T, jnp.sin(ang).T]
    return out


def kernel(x, norm_mix_pre, norm_mix_post, norm_mlp_pre, norm_mlp_post, w_in, a_q_norm, a_k_norm,
           b_cq_norm, b_ckv_norm, b_w_uq, b_w_ukv, c_lambda_q1, c_lambda_k1, c_lambda_q2,
           c_lambda_k2, d_rel_bias, g_out_a, g_out_b, g_out_c, g_out_d, w_out, w_up, w_down):
    B, S, D = x.shape
    depth = w_in.shape[0]
    col = lambda v: v.astype(F32).reshape(-1, 1)
    wt = lambda w: w.T.astype(BF16)
    tables = _rope_tables(S)
    slopes2 = jnp.exp2(-8.0 * jnp.arange(1, 5, dtype=F32) / 4) * LOG2E
    xT = jnp.transpose(x, (0, 2, 1))
    for l in range(depth):
        (qa, ka, va, qb, kb, vb, qc, kc, vc, qd, kd, vd, kna, knb, knc, knd) = _inproj(
            xT, col(norm_mix_pre[l]), wt(w_in[l]), col(a_q_norm[l]), col(a_k_norm[l]),
            col(b_cq_norm[l]), col(b_ckv_norm[l]), wt(b_w_uq[l]), wt(b_w_ukv[l]), tables)
        lambda_init = 0.8 - 0.6 * math.exp(-0.3 * l)
        lam_vecs = jnp.stack([c_lambda_q1[l], c_lambda_k1[l], c_lambda_q2[l], c_lambda_k2[l]]).astype(F32)
        oa = _attn_a(qa, ka, va, kna)
        ob = _attn_b(qb, kb, vb, knb)
        oc = _attn_c(qc, kc, vc, knc, slopes2, lam_vecs, col(g_out_c[l]), lambda_init)
        rel_bias = d_rel_bias[l].astype(F32)
        od = _attn_d(qd, kd, vd, knd, _na_bias_tables(rel_bias, S // GRID_W),
                     jnp.max(rel_bias, axis=(1, 2)) * LOG2E)
        xT = _post(xT, oa, ob, oc, od, col(g_out_a[l]), col(g_out_b[l]), col(g_out_d[l]),
                   wt(w_out[l]), col(norm_mix_post[l]), col(norm_mlp_pre[l]), wt(w_up[l]),
                   wt(w_down[l]), col(norm_mlp_post[l]))
    return jnp.transpose(xT, (0, 2, 1))
```

```python
import functools
import math

import numpy as np
import jax
import jax.numpy as jnp
from jax import lax
from jax.experimental import pallas as pl
from jax.experimental.pallas import tpu as pltpu

GRID_W = 64
HEAD_DIM = 64
GROUP_W = 256
N_KV_A = 2
MLA_Q_RANK = 192
MLA_KV_RANK = 128
MLA_NOPE = 64
MLA_ROPE = 32
DIFF_QK = 32
NA_ROWS = 8
NA_COLS = 16
ROPE_BASE = 10000.0
EPS = 1e-6
LOG2E = math.log2(math.e)
NEG_BIG = -1e30

A_COLS = 512
B_COLS = MLA_Q_RANK + MLA_KV_RANK + MLA_ROPE
C_COLS = 768
D_COLS = 768
OFF_B = A_COLS
OFF_C = OFF_B + B_COLS
OFF_D = OFF_C + C_COLS
IN_COLS = OFF_D + D_COLS

TOKEN_TILE = 512
MLP_TOKEN_TILE = 512
MLP_PART = 512
Q_TILE = 512
Q_SUBTILES = 4
KV_CHUNK = 256
QK_LOOKAHEAD = 2
KV_UNROLL = 16
NA_Q_TILE = 256
NA_SLAB_ROWS = 12
VMEM_LIMIT = 56 * 1024 * 1024

F32 = jnp.float32
BF16 = jnp.bfloat16


def _dot(a, b):
    return jnp.dot(a, b, preferred_element_type=F32)


def _rms_rows(x, gain):
    ms = jnp.mean(x * x, axis=0, keepdims=True)
    return x * lax.rsqrt(ms + EPS) * gain


def _rope_rows(x, cos, sin):
    n = cos.shape[0]
    x1, x2 = x[:n], x[n:]
    return x1 * cos - x2 * sin, x1 * sin + x2 * cos


def _inproj_kernel(x_ref, gpre_ref, w_ref, aq_ref, ak_ref, cqg_ref, ckvg_ref, wuq_ref, wukv_ref,
                   cr_ref, sr_ref, cc_ref, sc_ref, cs_ref, ss_ref,
                   qa_ref, ka_ref, va_ref, qb_ref, kb_ref, vb_ref,
                   qc_ref, kc_ref, vc_ref, qd_ref, kd_ref, vd_ref,
                   kna_ref, knb_ref, knc_ref, knd_ref):
    x = x_ref[0]
    h = _rms_rows(x, gpre_ref[...]).astype(BF16)
    cr, sr, cc, sc = cr_ref[...], sr_ref[...], cc_ref[...], sc_ref[...]
    cs, ss = cs_ref[...], ss_ref[...]
    tm = x.shape[1]

    def axial(t):
        r1, r2 = _rope_rows(t[0:32], cr, sr)
        c1, c2 = _rope_rows(t[32:64], cc, sc)
        return r1, r2, c1, c2

    def sumsq(t):
        return jnp.sum(t * t, axis=0, keepdims=True)

    ua = _dot(w_ref[0:A_COLS, :], h)
    qscale = HEAD_DIM ** -0.5 * LOG2E
    for hd in range(4):
        t = _rms_rows(ua[64 * hd:64 * hd + 64], aq_ref[...])
        for j, piece in enumerate(axial(t)):
            qa_ref[0, 64 * hd + 16 * j:64 * hd + 16 * j + 16, :] = (piece * qscale).astype(BF16)
    kparts = []
    for g in range(N_KV_A):
        t = _rms_rows(ua[256 + 64 * g:256 + 64 * g + 64], ak_ref[...])
        kparts.extend(axial(t))
        kna_ref[0, g] = sumsq(t)
    ka_ref[0] = jnp.concatenate(kparts, axis=0).T.astype(BF16)
    va_ref[0] = ua[384:512].astype(BF16)

    ub = _dot(w_ref[OFF_B:OFF_B + B_COLS, :], h)
    cqn = _rms_rows(ub[0:MLA_Q_RANK], cqg_ref[...]).astype(BF16)
    ckvn = _rms_rows(ub[MLA_Q_RANK:MLA_Q_RANK + MLA_KV_RANK], ckvg_ref[...]).astype(BF16)
    kr1, kr2 = _rope_rows(ub[MLA_Q_RANK + MLA_KV_RANK:B_COLS], cs, ss)
    qb = _dot(wuq_ref[...], cqn)
    kvb = _dot(wukv_ref[...], ckvn)
    bscale = (MLA_NOPE + MLA_ROPE) ** -0.5 * LOG2E
    zeros32 = jnp.zeros((32, tm), F32)
    for hd in range(4):
        base = 96 * hd
        q1, q2 = _rope_rows(qb[base + 64:base + 96], cs, ss)
        qb_ref[0, 128 * hd:128 * hd + 64, :] = (qb[base:base + 64] * bscale).astype(BF16)
        qb_ref[0, 128 * hd + 64:128 * hd + 80, :] = (q1 * bscale).astype(BF16)
        qb_ref[0, 128 * hd + 80:128 * hd + 96, :] = (q2 * bscale).astype(BF16)
        qb_ref[0, 128 * hd + 96:128 * hd + 128, :] = jnp.zeros((32, tm), BF16)
        kfull = jnp.concatenate([kvb[128 * hd:128 * hd + 64], kr1, kr2, zeros32], axis=0)
        kb_ref[0, hd] = kfull.T.astype(BF16)
        knb_ref[0, hd] = sumsq(kfull)
        vb_ref[0, 64 * hd:64 * hd + 64, :] = kvb[128 * hd + 64:128 * hd + 128].astype(BF16)

    uc = _dot(w_ref[OFF_C:OFF_C + C_COLS, :], h)
    qc_ref[0] = (uc[0:256] * (DIFF_QK ** -0.5 * LOG2E)).astype(BF16)
    for p in range(2):
        kc_ref[0, p] = uc[256 + 128 * p:256 + 128 * p + 128].T.astype(BF16)
    vc_ref[0] = uc[512:768].astype(BF16)
    for hd in range(4):
        for mp in range(2):
            base = 256 + 64 * hd + 32 * mp
            knc_ref[0, hd, mp:mp + 1, :] = sumsq(uc[base:base + 32])

    ud = _dot(w_ref[OFF_D:OFF_D + D_COLS, :], h)
    qd_ref[0] = (ud[0:256] * (HEAD_DIM ** -0.5 * LOG2E)).astype(BF16)
    for p in range(2):
        kd_ref[0, p] = ud[256 + 128 * p:256 + 128 * p + 128].T.astype(BF16)
    vd_ref[0] = ud[512:768].astype(BF16)
    for hd in range(4):
        knd_ref[0, hd] = sumsq(ud[256 + 64 * hd:256 + 64 * hd + 64])


def _const_spec(shape):
    nd = len(shape)
    return pl.BlockSpec(shape, lambda *_: (0,) * nd, pipeline_mode=pl.Buffered(1))


def _inproj(xT, gpre, wT, aq, ak, cqg, ckvg, wuqT, wukvT, tables):
    B, D, S = xT.shape
    tm = min(TOKEN_TILE, S)
    grid = (B, S // tm)
    tab_spec = pl.BlockSpec((16, tm), lambda b, i: (0, i))
    rows_spec = lambda rows: pl.BlockSpec((1, rows, tm), lambda b, i: (b, 0, i))
    out_shape = (
        jax.ShapeDtypeStruct((B, 256, S), BF16),
        jax.ShapeDtypeStruct((B, S, 128), BF16),
        jax.ShapeDtypeStruct((B, 128, S), BF16),
        jax.ShapeDtypeStruct((B, 512, S), BF16),
        jax.ShapeDtypeStruct((B, 4, S, 128), BF16),
        jax.ShapeDtypeStruct((B, 256, S), BF16),
        jax.ShapeDtypeStruct((B, 256, S), BF16),
        jax.ShapeDtypeStruct((B, 2, S, 128), BF16),
        jax.ShapeDtypeStruct((B, 256, S), BF16),
        jax.ShapeDtypeStruct((B, 256, S), BF16),
        jax.ShapeDtypeStruct((B, 2, S, 128), BF16),
        jax.ShapeDtypeStruct((B, 256, S), BF16),
        jax.ShapeDtypeStruct((B, 2, 1, S), F32),
        jax.ShapeDtypeStruct((B, 4, 1, S), F32),
        jax.ShapeDtypeStruct((B, 4, 2, S), F32),
        jax.ShapeDtypeStruct((B, 4, 1, S), F32),
    )
    kspec = lambda n: pl.BlockSpec((1, n, tm, 128), lambda b, i: (b, 0, i, 0))
    nspec = lambda n, r: pl.BlockSpec((1, n, r, tm), lambda b, i: (b, 0, 0, i))
    out_specs = (
        rows_spec(256), pl.BlockSpec((1, tm, 128), lambda b, i: (b, i, 0)), rows_spec(128),
        rows_spec(512), kspec(4), rows_spec(256),
        rows_spec(256), kspec(2), rows_spec(256),
        rows_spec(256), kspec(2), rows_spec(256),
        nspec(2, 1), nspec(4, 1), nspec(4, 2), nspec(4, 1),
    )
    in_specs = [
        rows_spec(D), _const_spec(gpre.shape), _const_spec(wT.shape),
        _const_spec(aq.shape), _const_spec(ak.shape), _const_spec(cqg.shape), _const_spec(ckvg.shape),
        _const_spec(wuqT.shape), _const_spec(wukvT.shape),
    ] + [tab_spec] * 6
    return pl.pallas_call(
        _inproj_kernel, grid=grid, in_specs=in_specs, out_specs=out_specs, out_shape=out_shape,
        compiler_params=pltpu.CompilerParams(vmem_limit_bytes=VMEM_LIMIT),
        name="inproj",
    )(xT, gpre, wT, aq, ak, cqg, ckvg, wuqT, wukvT, *tables)


def _pad_rows(q32, slot, nslots):
    return jnp.concatenate([jnp.where(slot == j, q32, 0.0) for j in range(nslots)], axis=0)


def _softmax_stage(t, cst, m):
    m_new = jnp.maximum(m, jnp.max(t, axis=0, keepdims=True) + cst)
    alpha = jnp.exp2(m - m_new)
    p = jnp.exp2(t - (m_new - cst)).astype(BF16)
    return m_new, alpha, p


SHIFT_OK_MIN_SUM = 2.0 ** -64


def _attention(nchunks, tq, shifts, qk_stage, pv_stage, finalize):
    n_sub, n_maps = len(shifts), len(shifts[0])
    dacc = pv_stage.rows
    zero_acc = lambda: tuple(jnp.zeros((dacc, tq), F32) for _ in range(n_maps))

    def run_tasks(tasks, accs):
        queue = []
        for idx, (sub, c, j) in enumerate(tasks):
            while len(queue) + idx < min(idx + QK_LOOKAHEAD + 1, len(tasks)):
                ahead_sub, ahead_c, ahead_j = tasks[idx + len(queue)]
                queue.append(qk_stage(ahead_c, ahead_j, ahead_sub))
            tile, cst = queue.pop(0)
            p = jnp.exp2(tile - (shifts[sub][j] - cst)).astype(BF16)
            accs[sub][j] = accs[sub][j] + pv_stage(c, p)

    unroll = math.gcd(nchunks, KV_UNROLL)
    accs = [list(zero_acc()) for _ in range(n_sub)]
    if unroll == nchunks:
        run_tasks([(sub, c, j) for sub in range(n_sub) for c in range(nchunks)
                   for j in range(n_maps)], accs)
    else:
        for sub in range(n_sub):
            def trip(t, acc, sub=sub):
                acc = {sub: list(acc)}
                run_tasks([(sub, t * unroll + u, j) for u in range(unroll) for j in range(n_maps)], acc)
                return tuple(acc[sub])
            accs[sub] = list(lax.fori_loop(0, nchunks // unroll, trip, tuple(accs[sub])))

    for sub in range(n_sub):
        finalize(sub, accs[sub])
        lmin = functools.reduce(jnp.minimum, [jnp.min(a[dacc - 16:dacc - 15]) for a in accs[sub]])

        @pl.when(jnp.logical_not(lmin >= SHIFT_OK_MIN_SUM))
        def _(sub=sub):
            def online_trip(c, carry):
                m, acc = carry
                out = []
                for j in range(n_maps):
                    tile, cst = qk_stage(c, j, sub)
                    m_new, alpha, p = _softmax_stage(tile, cst, m[j])
                    out.append((m_new, alpha * acc[j] + pv_stage(c, p)))
                return tuple(o[0] for o in out), tuple(o[1] for o in out)

            m0 = tuple(jnp.full((1, tq), NEG_BIG, F32) for _ in range(n_maps))
            finalize(sub, lax.fori_loop(0, nchunks, online_trip, (m0, zero_acc()))[1])


def _chunk_slice(c, tk):
    return pl.ds(c * tk if isinstance(c, int) else pl.multiple_of(c * tk, tk), tk)


def _make_pv_stage(v_ref, tk):
    dv = v_ref.shape[1]
    ones = jnp.ones((16, tk), BF16)

    def pv_stage(c, p):
        vaug = jnp.concatenate([v_ref[0, :, _chunk_slice(c, tk)], ones], axis=0)
        return _dot(vaug, p)

    pv_stage.rows = dv + 16
    return pv_stage


def _score_bound(q32, kn2):
    return jnp.sqrt(jnp.sum(q32 * q32, axis=0, keepdims=True) * jnp.max(kn2, axis=1, keepdims=True))


def _attn_kernel(q_ref, k_ref, v_ref, kn_ref, o_ref, *, k_slots, nchunks, tk, tq):
    n_sub = q_ref.shape[2] // tq
    dv = v_ref.shape[1]
    qp, shifts = [], []
    for sub in range(n_sub):
        q32 = q_ref[0, :, sub * tq:(sub + 1) * tq].astype(F32)
        shifts.append((_score_bound(q32, kn_ref[0, 0]),))
        if k_slots > 1:
            q32 = _pad_rows(q32, pl.program_id(1) // 2, k_slots)
        qp.append(q32.astype(BF16))
    k_at = (lambda sl: k_ref[0, sl, :]) if len(k_ref.shape) == 3 else (lambda sl: k_ref[0, 0, sl, :])

    def qk_stage(c, j, sub):
        return _dot(k_at(_chunk_slice(c, tk)), qp[sub]), jnp.float32(0.0)

    def finalize(sub, acc):
        o_ref[0, :, sub * tq:(sub + 1) * tq] = acc[0][0:dv] / acc[0][dv:dv + 1]

    _attention(nchunks, tq, shifts, qk_stage, _make_pv_stage(v_ref, tk), finalize)


def _attn_a(qT, k, vT, kn):
    B, _, S = qT.shape
    tq, tk = min(Q_TILE, S), min(KV_CHUNK, S)
    tstep = tq * math.gcd(S // tq, Q_SUBTILES)
    return pl.pallas_call(
        functools.partial(_attn_kernel, k_slots=2, nchunks=S // tk, tk=tk, tq=tq),
        grid=(B, 4, S // tstep),
        in_specs=[pl.BlockSpec((1, 64, tstep), lambda b, h, i: (b, h, i)),
                  pl.BlockSpec((1, S, 128), lambda b, h, i: (b, 0, 0)),
                  pl.BlockSpec((1, 64, S), lambda b, h, i: (b, h // 2, 0)),
                  pl.BlockSpec((1, 1, 1, S), lambda b, h, i: (b, h // 2, 0, 0))],
        out_specs=pl.BlockSpec((1, 64, tstep), lambda b, h, i: (b, h, i)),
        out_shape=jax.ShapeDtypeStruct((B, 256, S), F32),
        compiler_params=pltpu.CompilerParams(vmem_limit_bytes=VMEM_LIMIT),
        name="attn_gqa",
    )(qT, k, vT, kn)


def _attn_b(qT, k, vT, kn):
    B, _, S = qT.shape
    tq, tk = min(Q_TILE, S), min(KV_CHUNK, S)
    tstep = tq * math.gcd(S // tq, Q_SUBTILES)
    return pl.pallas_call(
        functools.partial(_attn_kernel, k_slots=1, nchunks=S // tk, tk=tk, tq=tq),
        grid=(B, 4, S // tstep),
        in_specs=[pl.BlockSpec((1, 128, tstep), lambda b, h, i: (b, h, i)),
                  pl.BlockSpec((1, 1, S, 128), lambda b, h, i: (b, h, 0, 0)),
                  pl.BlockSpec((1, 64, S), lambda b, h, i: (b, h, 0)),
                  pl.BlockSpec((1, 1, 1, S), lambda b, h, i: (b, h, 0, 0))],
        out_specs=pl.BlockSpec((1, 64, tstep), lambda b, h, i: (b, h, i)),
        out_shape=jax.ShapeDtypeStruct((B, 256, S), F32),
        compiler_params=pltpu.CompilerParams(vmem_limit_bytes=VMEM_LIMIT),
        name="attn_mla",
    )(qT, k, vT, kn)


def _diff_kernel(slope_ref, lam_ref, q_ref, k_ref, v_ref, kn_ref, g_ref, o_ref, bias_ref,
                 *, nchunks, tk, tq, out_scale, lambda_init):
    h = pl.program_id(1)
    i = pl.program_id(2)
    n_sub = q_ref.shape[2] // tq
    dv = v_ref.shape[1]
    n_inside = tq // tk
    slope = slope_ref[h]

    @pl.when(i == 0)
    def _():
        d = (lax.broadcasted_iota(jnp.int32, (tk, tq), 1)
             - lax.broadcasted_iota(jnp.int32, (tk, tq), 0)).astype(F32)
        bias_ref[0] = -d * slope
        bias_ref[1] = d * slope
        for j in range(n_inside):
            bias_ref[2 + j] = -jnp.abs(d - float(j * tk)) * slope

    slot0 = (h % 2) * 2
    qp, shifts = [], []
    for sub in range(n_sub):
        q32 = q_ref[0, :, sub * tq:(sub + 1) * tq].astype(F32)
        maps = [q32[32 * mp:32 * mp + 32] for mp in range(2)]
        qp.append([_pad_rows(maps[mp], slot0 + mp, 4).astype(BF16) for mp in range(2)])
        shifts.append(tuple(_score_bound(maps[mp], kn_ref[0, 0, mp:mp + 1, :]) for mp in range(2)))

    def qk_stage(c, mp, sub):
        kc = k_ref[0, 0, _chunk_slice(c, tk), :]
        gap = (i * n_sub + sub) * tq - c * tk
        inside = jnp.logical_and(gap <= 0, gap > -tq)
        bias = bias_ref[jnp.where(gap > 0, 0, jnp.where(inside, 2 + (-gap) // tk, 1))]
        cst = jnp.where(inside, 0.0, -jnp.abs(gap).astype(F32) * slope)
        return _dot(kc, qp[sub][mp]) + bias, cst

    def finalize(sub, acc):
        a1, a2 = acc
        lv = lam_ref[...]
        lam = (jnp.exp(jnp.sum(lv[0:1] * lv[1:2], axis=1, keepdims=True))
               - jnp.exp(jnp.sum(lv[2:3] * lv[3:4], axis=1, keepdims=True)) + lambda_init)
        o = a1[0:dv] / a1[dv:dv + 1] - lam * (a2[0:dv] / a2[dv:dv + 1])
        o_ref[0, :, sub * tq:(sub + 1) * tq] = _rms_rows(o, g_ref[...]) * out_scale

    _attention(nchunks, tq, shifts, qk_stage, _make_pv_stage(v_ref, tk), finalize)


def _attn_c(qT, k, vT, kn, slopes2, lam_vecs, g_head, lambda_init):
    B, _, S = qT.shape
    tq, tk = min(Q_TILE, S), min(KV_CHUNK, S)
    tstep = tq * math.gcd(S // tq, Q_SUBTILES)
    smem = pl.BlockSpec(memory_space=pltpu.SMEM)
    return pl.pallas_call(
        functools.partial(_diff_kernel, nchunks=S // tk, tk=tk, tq=tq,
                          out_scale=1.0 - lambda_init, lambda_init=lambda_init),
        grid=(B, 4, S // tstep),
        in_specs=[smem,
                  pl.BlockSpec((4, 32), lambda b, h, i: (0, 0)),
                  pl.BlockSpec((1, 64, tstep), lambda b, h, i: (b, h, i)),
                  pl.BlockSpec((1, 1, S, 128), lambda b, h, i: (b, h // 2, 0, 0)),
                  pl.BlockSpec((1, 64, S), lambda b, h, i: (b, h, 0)),
                  pl.BlockSpec((1, 1, 2, S), lambda b, h, i: (b, h, 0, 0)),
                  pl.BlockSpec((64, 1), lambda b, h, i: (0, 0))],
        out_specs=pl.BlockSpec((1, 64, tstep), lambda b, h, i: (b, h, i)),
        out_shape=jax.ShapeDtypeStruct((B, 256, S), F32),
        scratch_shapes=[pltpu.VMEM((2 + tq // tk, tk, tq), F32)],
        compiler_params=pltpu.CompilerParams(vmem_limit_bytes=VMEM_LIMIT),
        name="attn_diff",
    )(slopes2, lam_vecs, qT, k, vT, kn, g_head)


def _na_kernel(bmax_ref, q_ref, k_ref, v_ref, kn_ref, bias_ref, o_ref, *, n_rows):
    h = pl.program_id(0)
    n_blocks = n_rows // 4
    nk = NA_SLAB_ROWS * GRID_W
    tq = NA_Q_TILE
    dv = v_ref.shape[1]
    ones = jnp.ones((16, nk), BF16)
    kmax2 = jnp.max(kn_ref[0, 0], axis=1, keepdims=True)
    bmax = bmax_ref[h]

    def variant(a):
        return 0 if a == 0 else (2 if a == n_blocks - 1 else 1)

    def slab(a):
        return min(max(4 * a - 4, 0), n_rows - NA_SLAB_ROWS) * GRID_W

    def q_block(a):
        return q_ref[0, :, a * tq:(a + 1) * tq].astype(F32)

    def scores(a):
        qp = _pad_rows(q_block(a), h % 2, 2).astype(BF16)
        return _dot(k_ref[0, 0, slab(a):slab(a) + nk, :], qp) + bias_ref[0, variant(a)]

    def pv(a, p):
        vaug = jnp.concatenate([v_ref[0, :, slab(a):slab(a) + nk], ones], axis=0)
        return _dot(vaug, p)

    lmin = None
    nxt = scores(0)
    for a in range(n_blocks):
        s = nxt
        if a + 1 < n_blocks:
            nxt = scores(a + 1)
        q32 = q_block(a)
        shift = jnp.sqrt(jnp.sum(q32 * q32, axis=0, keepdims=True) * kmax2) + bmax
        acc = pv(a, jnp.exp2(s - shift).astype(BF16))
        o_ref[0, :, a * tq:(a + 1) * tq] = acc[0:dv] / acc[dv:dv + 1]
        amin = jnp.min(acc[dv:dv + 1])
        lmin = amin if lmin is None else jnp.minimum(lmin, amin)

    @pl.when(jnp.logical_not(lmin >= SHIFT_OK_MIN_SUM))
    def _():
        for a in range(n_blocks):
            s = scores(a)
            p = jnp.exp2(s - jnp.max(s, axis=0, keepdims=True)).astype(BF16)
            acc = pv(a, p)
            o_ref[0, :, a * tq:(a + 1) * tq] = acc[0:dv] / acc[dv:dv + 1]


def _na_bias_tables(rel_bias, n_rows):
    n_blocks = n_rows // 4
    kr_w = min(NA_ROWS, n_rows)
    col = np.arange(GRID_W)
    c0 = np.clip(col - NA_COLS // 2, 0, GRID_W - NA_COLS)
    col_ok = (col[:, None] >= c0[None, :]) & (col[:, None] < c0[None, :] + NA_COLS)
    dc = col[:, None] - col[None, :] + (NA_COLS - 1)
    onehot_c = ((dc[None] == np.arange(2 * NA_COLS - 1)[:, None, None]) & col_ok[None]).astype(np.float32)

    def row_geometry(a):
        R0 = int(np.clip(4 * a - 4, 0, n_rows - NA_SLAB_ROWS))
        r = 4 * a + np.arange(4)
        r0 = np.clip(r - kr_w // 2, 0, n_rows - kr_w)
        kr = R0 + np.arange(NA_SLAB_ROWS)
        ok = (kr[:, None] >= r0[None, :]) & (kr[:, None] < r0[None, :] + kr_w)
        dr = kr[:, None] - r[None, :] + (NA_ROWS - 1)
        return ((dr[:, :, None] == np.arange(2 * NA_ROWS - 1)) & ok[:, :, None]), ok

    variants = [row_geometry(0), row_geometry(1), row_geometry(n_blocks - 1)]
    for a in range(1, n_blocks - 1):
        assert np.array_equal(row_geometry(a)[0], variants[1][0])
    onehot_r = np.stack([v[0] for v in variants]).astype(np.float32)
    valid = (np.stack([v[1] for v in variants])[:, :, None, :, None]
             & col_ok[None, None, :, None, :])
    assert (valid.reshape(3, -1, NA_Q_TILE).sum(axis=1) == kr_w * NA_COLS).all()
    hi = lax.Precision.HIGHEST
    t1 = jnp.einsum('hrd,dkc->hrkc', rel_bias, onehot_c, precision=hi)
    tab = jnp.einsum('viqr,hrkc->hvikqc', onehot_r, t1, precision=hi) * LOG2E
    tab = jnp.where(valid[None], tab, NEG_BIG)
    return tab.reshape(4, 3, NA_SLAB_ROWS * GRID_W, NA_Q_TILE).astype(F32)


def _attn_d(qT, k, vT, kn, bias_tab, bias_max):
    B, _, S = qT.shape
    n_rows = S // GRID_W
    nk = NA_SLAB_ROWS * GRID_W
    return pl.pallas_call(
        functools.partial(_na_kernel, n_rows=n_rows),
        grid=(4, B),
        in_specs=[pl.BlockSpec(memory_space=pltpu.SMEM),
                  pl.BlockSpec((1, 64, S), lambda h, b: (b, h, 0)),
                  pl.BlockSpec((1, 1, S, 128), lambda h, b: (b, h // 2, 0, 0)),
                  pl.BlockSpec((1, 64, S), lambda h, b: (b, h, 0)),
                  pl.BlockSpec((1, 1, 1, S), lambda h, b: (b, h, 0, 0)),
                  pl.BlockSpec((1, 3, nk, NA_Q_TILE), lambda h, b: (h, 0, 0, 0))],
        out_specs=pl.BlockSpec((1, 64, S), lambda h, b: (b, h, 0)),
        out_shape=jax.ShapeDtypeStruct((B, 256, S), F32),
        compiler_params=pltpu.CompilerParams(vmem_limit_bytes=VMEM_LIMIT),
        name="attn_nbr",
    )(bias_max, qT, k, vT, kn, bias_tab)


def _post_kernel(x_ref, oa_ref, ob_ref, oc_ref, od_ref, ga_ref, gb_ref, gd_ref,
                 wout_ref, gpost_ref, gmlp_ref, wup_ref, wdown_ref, gmlp_post_ref, y_ref,
                 *, ff_chunk, n_parts):
    tm = x_ref.shape[2]
    parts = [slice(s * tm // n_parts, (s + 1) * tm // n_parts) for s in range(n_parts)]
    n_ff = wup_ref.shape[0] // ff_chunk

    def mixer_out(cols):
        cat = jnp.concatenate([
            _rms_rows(oa_ref[0, :, cols], ga_ref[...]).astype(BF16),
            _rms_rows(ob_ref[0, :, cols], gb_ref[...]).astype(BF16),
            oc_ref[0, :, cols].astype(BF16),
            _rms_rows(od_ref[0, :, cols], gd_ref[...]).astype(BF16)], axis=0)
        return _dot(wout_ref[...], cat)

    def mlp(h):
        f = None
        for j in range(n_ff):
            up = _dot(wup_ref[j * ff_chunk:(j + 1) * ff_chunk, :], h)
            act = jnp.square(jnp.maximum(up, 0.0)).astype(BF16)
            down = _dot(wdown_ref[:, j * ff_chunk:(j + 1) * ff_chunk], act)
            f = down if f is None else f + down
        return f

    mix = [mixer_out(cols) for cols in parts]
    x1, f = [], []
    for s, cols in enumerate(parts):
        x1.append(x_ref[0, :, cols] + _rms_rows(mix[s], gpost_ref[...]))
        f.append(mlp(_rms_rows(x1[s], gmlp_ref[...]).astype(BF16)))
    for s, cols in enumerate(parts):
        y_ref[0, :, cols] = x1[s] + _rms_rows(f[s], gmlp_post_ref[...])


def _post(xT, oa, ob, oc, od, ga, gb, gd, woutT, gpost, gmlp, wupT, wdownT, gmlp_post):
    B, D, S = xT.shape
    tm = min(MLP_TOKEN_TILE, S)
    rows_spec = lambda rows: pl.BlockSpec((1, rows, tm), lambda b, i: (b, 0, i))
    consts = (ga, gb, gd, woutT, gpost, gmlp, wupT, wdownT, gmlp_post)
    return pl.pallas_call(
        functools.partial(_post_kernel, ff_chunk=1024, n_parts=tm // math.gcd(tm, MLP_PART)),
        grid=(B, S // tm),
        in_specs=[rows_spec(D)] + [rows_spec(256)] * 4 + [_const_spec(c.shape) for c in consts],
        out_specs=rows_spec(D),
        out_shape=jax.ShapeDtypeStruct((B, D, S), F32),
        compiler_params=pltpu.CompilerParams(vmem_limit_bytes=VMEM_LIMIT),
        name="outproj_mlp",
    )(xT, oa, ob, oc, od, *consts)


def _rope_tables(n_tokens):
    def angles(pos, dim):
        inv = ROPE_BASE ** (-jnp.arange(0, dim, 2, dtype=F32) / dim)
        return pos.astype(F32)[:, None] * inv[None, :]

    t = jnp.arange(n_tokens, dtype=jnp.int32)
    out = []
    for ang in (angles(t // GRID_W, HEAD_DIM // 2), angles(t % GRID_W, HEAD_DIM // 2),
                angles(t, MLA_ROPE)):
        out += [jnp.cos(ang).T, jnp.sin(ang).T]
    return out


def kernel(x, norm_mix_pre, norm_mix_post, norm_mlp_pre, norm_mlp_post, w_in, a_q_norm, a_k_norm,
           b_cq_norm, b_ckv_norm, b_w_uq, b_w_ukv, c_lambda_q1, c_lambda_k1, c_lambda_q2,
           c_lambda_k2, d_rel_bias, g_out_a, g_out_b, g_out_c, g_out_d, w_out, w_up, w_down):
    B, S, D = x.shape
    depth = w_in.shape[0]
    col = lambda v: v.astype(F32).reshape(-1, 1)
    wt = lambda w: w.T.astype(BF16)
    tables = _rope_tables(S)
    slopes2 = jnp.exp2(-8.0 * jnp.arange(1, 5, dtype=F32) / 4) * LOG2E
    xT = jnp.transpose(x, (0, 2, 1))
    for l in range(depth):
        (qa, ka, va, qb, kb, vb, qc, kc, vc, qd, kd, vd, kna, knb, knc, knd) = _inproj(
            xT, col(norm_mix_pre[l]), wt(w_in[l]), col(a_q_norm[l]), col(a_k_norm[l]),
            col(b_cq_norm[l]), col(b_ckv_norm[l]), wt(b_w_uq[l]), wt(b_w_ukv[l]), tables)
        lambda_init = 0.8 - 0.6 * math.exp(-0.3 * l)
        lam_vecs = jnp.stack([c_lambda_q1[l], c_lambda_k1[l], c_lambda_q2[l], c_lambda_k2[l]]).astype(F32)
        oa = _attn_a(qa, ka, va, kna)
        ob = _attn_b(qb, kb, vb, knb)
        oc = _attn_c(qc, kc, vc, knc, slopes2, lam_vecs, col(g_out_c[l]), lambda_init)
        rel_bias = d_rel_bias[l].astype(F32)
        od = _attn_d(qd, kd, vd, knd, _na_bias_tables(rel_bias, S // GRID_W),
                     jnp.max(rel_bias, axis=(1, 2)) * LOG2E)
        xT = _post(xT, oa, ob, oc, od, col(g_out_a[l]), col(g_out_b[l]), col(g_out_d[l]),
                   wt(w_out[l]), col(norm_mix_post[l]), col(norm_mlp_pre[l]), wt(w_up[l]),
                   wt(w_down[l]), col(norm_mlp_post[l]))
    return jnp.transpose(xT, (0, 2, 1))
```

```python
import functools
import math

import numpy as np
import jax
import jax.numpy as jnp
from jax import lax
from jax.experimental import pallas as pl
from jax.experimental.pallas import tpu as pltpu

GRID_W = 64
HEAD_DIM = 64
GROUP_W = 256
N_KV_A = 2
MLA_Q_RANK = 192
MLA_KV_RANK = 128
MLA_NOPE = 64
MLA_ROPE = 32
DIFF_QK = 32
NA_ROWS = 8
NA_COLS = 16
ROPE_BASE = 10000.0
EPS = 1e-6
LOG2E = math.log2(math.e)
NEG_BIG = -1e30

A_COLS = 512
B_COLS = MLA_Q_RANK + MLA_KV_RANK + MLA_ROPE
C_COLS = 768
D_COLS = 768
OFF_B = A_COLS
OFF_C = OFF_B + B_COLS
OFF_D = OFF_C + C_COLS
IN_COLS = OFF_D + D_COLS

TOKEN_TILE = 512
MLP_TOKEN_TILE = 512
MLP_PART = 512
Q_TILE = 512
Q_SUBTILES = 4
KV_CHUNK = 256
QK_LOOKAHEAD = 2
KV_UNROLL = 16
NA_Q_TILE = 256
NA_SLAB_ROWS = 12
VMEM_LIMIT = 56 * 1024 * 1024

F32 = jnp.float32
BF16 = jnp.bfloat16


def _dot(a, b):
    return jnp.dot(a, b, preferred_element_type=F32)


def _rms_rows(x, gain):
    ms = jnp.mean(x * x, axis=0, keepdims=True)
    return x * lax.rsqrt(ms + EPS) * gain


def _rope_rows(x, cos, sin):
    n = cos.shape[0]
    x1, x2 = x[:n], x[n:]
    return x1 * cos - x2 * sin, x1 * sin + x2 * cos


def _inproj_kernel(x_ref, gpre_ref, w_ref, aq_ref, ak_ref, cqg_ref, ckvg_ref, wuq_ref, wukv_ref,
                   cr_ref, sr_ref, cc_ref, sc_ref, cs_ref, ss_ref,
                   qa_ref, ka_ref, va_ref, qb_ref, kb_ref, vb_ref,
                   qc_ref, kc_ref, vc_ref, qd_ref, kd_ref, vd_ref,
                   kna_ref, knb_ref, knc_ref, knd_ref):
    x = x_ref[0]
    h = _rms_rows(x, gpre_ref[...]).astype(BF16)
    cr, sr, cc, sc = cr_ref[...], sr_ref[...], cc_ref[...], sc_ref[...]
    cs, ss = cs_ref[...], ss_ref[...]
    tm = x.shape[1]

    def axial(t):
        r1, r2 = _rope_rows(t[0:32], cr, sr)
        c1, c2 = _rope_rows(t[32:64], cc, sc)
        return r1, r2, c1, c2

    def sumsq(t):
        return jnp.sum(t * t, axis=0, keepdims=True)

    ua = _dot(w_ref[0:A_COLS, :], h)
    qscale = HEAD_DIM ** -0.5 * LOG2E
    for hd in range(4):
        t = _rms_rows(ua[64 * hd:64 * hd + 64], aq_ref[...])
        for j, piece in enumerate(axial(t)):
            qa_ref[0, 64 * hd + 16 * j:64 * hd + 16 * j + 16, :] = (piece * qscale).astype(BF16)
    kparts = []
    for g in range(N_KV_A):
        t = _rms_rows(ua[256 + 64 * g:256 + 64 * g + 64], ak_ref[...])
        kparts.extend(axial(t))
        kna_ref[0, g] = sumsq(t)
    ka_ref[0] = jnp.concatenate(kparts, axis=0).T.astype(BF16)
    va_ref[0] = ua[384:512].astype(BF16)

    ub = _dot(w_ref[OFF_B:OFF_B + B_COLS, :], h)
    cqn = _rms_rows(ub[0:MLA_Q_RANK], cqg_ref[...]).astype(BF16)
    ckvn = _rms_rows(ub[MLA_Q_RANK:MLA_Q_RANK + MLA_KV_RANK], ckvg_ref[...]).astype(BF16)
    kr1, kr2 = _rope_rows(ub[MLA_Q_RANK + MLA_KV_RANK:B_COLS], cs, ss)
    qb = _dot(wuq_ref[...], cqn)
    kvb = _dot(wukv_ref[...], ckvn)
    bscale = (MLA_NOPE + MLA_ROPE) ** -0.5 * LOG2E
    zeros32 = jnp.zeros((32, tm), F32)
    for hd in range(4):
        base = 96 * hd
        q1, q2 = _rope_rows(qb[base + 64:base + 96], cs, ss)
        qb_ref[0, 128 * hd:128 * hd + 64, :] = (qb[base:base + 64] * bscale).astype(BF16)
        qb_ref[0, 128 * hd + 64:128 * hd + 80, :] = (q1 * bscale).astype(BF16)
        qb_ref[0, 128 * hd + 80:128 * hd + 96, :] = (q2 * bscale).astype(BF16)
        qb_ref[0, 128 * hd + 96:128 * hd + 128, :] = jnp.zeros((32, tm), BF16)
        kfull = jnp.concatenate([kvb[128 * hd:128 * hd + 64], kr1, kr2, zeros32], axis=0)
        kb_ref[0, hd] = kfull.T.astype(BF16)
        knb_ref[0, hd] = sumsq(kfull)
        vb_ref[0, 64 * hd:64 * hd + 64, :] = kvb[128 * hd + 64:128 * hd + 128].astype(BF16)

    uc = _dot(w_ref[OFF_C:OFF_C + C_COLS, :], h)
    qc_ref[0] = (uc[0:256] * (DIFF_QK ** -0.5 * LOG2E)).astype(BF16)
    for p in range(2):
        kc_ref[0, p] = uc[256 + 128 * p:256 + 128 * p + 128].T.astype(BF16)
    vc_ref[0] = uc[512:768].astype(BF16)
    for hd in range(4):
        for mp in range(2):
            base = 256 + 64 * hd + 32 * mp
            knc_ref[0, hd, mp:mp + 1, :] = sumsq(uc[base:base + 32])

    ud = _dot(w_ref[OFF_D:OFF_D + D_COLS, :], h)
    qd_ref[0] = (ud[0:256] * (HEAD_DIM ** -0.5 * LOG2E)).astype(BF16)
    for p in range(2):
        kd_ref[0, p] = ud[256 + 128 * p:256 + 128 * p + 128].T.astype(BF16)
    vd_ref[0] = ud[512:768].astype(BF16)
    for hd in range(4):
        knd_ref[0, hd] = sumsq(ud[256 + 64 * hd:256 + 64 * hd + 64])


def _const_spec(shape):
    nd = len(shape)
    return pl.BlockSpec(shape, lambda *_: (0,) * nd, pipeline_mode=pl.Buffered(1))


def _inproj(xT, gpre, wT, aq, ak, cqg, ckvg, wuqT, wukvT, tables):
    B, D, S = xT.shape
    tm = min(TOKEN_TILE, S)
    grid = (B, S // tm)
    tab_spec = pl.BlockSpec((16, tm), lambda b, i: (0, i))
    rows_spec = lambda rows: pl.BlockSpec((1, rows, tm), lambda b, i: (b, 0, i))
    out_shape = (
        jax.ShapeDtypeStruct((B, 256, S), BF16),
        jax.ShapeDtypeStruct((B, S, 128), BF16),
        jax.ShapeDtypeStruct((B, 128, S), BF16),
        jax.ShapeDtypeStruct((B, 512, S), BF16),
        jax.ShapeDtypeStruct((B, 4, S, 128), BF16),
        jax.ShapeDtypeStruct((B, 256, S), BF16),
        jax.ShapeDtypeStruct((B, 256, S), BF16),
        jax.ShapeDtypeStruct((B, 2, S, 128), BF16),
        jax.ShapeDtypeStruct((B, 256, S), BF16),
        jax.ShapeDtypeStruct((B, 256, S), BF16),
        jax.ShapeDtypeStruct((B, 2, S, 128), BF16),
        jax.ShapeDtypeStruct((B, 256, S), BF16),
        jax.ShapeDtypeStruct((B, 2, 1, S), F32),
        jax.ShapeDtypeStruct((B, 4, 1, S), F32),
        jax.ShapeDtypeStruct((B, 4, 2, S), F32),
        jax.ShapeDtypeStruct((B, 4, 1, S), F32),
    )
    kspec = lambda n: pl.BlockSpec((1, n, tm, 128), lambda b, i: (b, 0, i, 0))
    nspec = lambda n, r: pl.BlockSpec((1, n, r, tm), lambda b, i: (b, 0, 0, i))
    out_specs = (
        rows_spec(256), pl.BlockSpec((1, tm, 128), lambda b, i: (b, i, 0)), rows_spec(128),
        rows_spec(512), kspec(4), rows_spec(256),
        rows_spec(256), kspec(2), rows_spec(256),
        rows_spec(256), kspec(2), rows_spec(256),
        nspec(2, 1), nspec(4, 1), nspec(4, 2), nspec(4, 1),
    )
    in_specs = [
        rows_spec(D), _const_spec(gpre.shape), _const_spec(wT.shape),
        _const_spec(aq.shape), _const_spec(ak.shape), _const_spec(cqg.shape), _const_spec(ckvg.shape),
        _const_spec(wuqT.shape), _const_spec(wukvT.shape),
    ] + [tab_spec] * 6
    return pl.pallas_call(
        _inproj_kernel, grid=grid, in_specs=in_specs, out_specs=out_specs, out_shape=out_shape,
        compiler_params=pltpu.CompilerParams(vmem_limit_bytes=VMEM_LIMIT),
        name="inproj",
    )(xT, gpre, wT, aq, ak, cqg, ckvg, wuqT, wukvT, *tables)


def _pad_rows(q32, slot, nslots):
    return jnp.concatenate([jnp.where(slot == j, q32, 0.0) for j in range(nslots)], axis=0)


def _softmax_stage(t, cst, m):
    m_new = jnp.maximum(m, jnp.max(t, axis=0, keepdims=True) + cst)
    alpha = jnp.exp2(m - m_new)
    p = jnp.exp2(t - (m_new - cst)).astype(BF16)
    return m_new, alpha, p


SHIFT_OK_MIN_SUM = 2.0 ** -64
ZERO_PROB_LOG2 = 140.0


def _attention(nchunks, tq, shifts, qk_stage, pv_stage, finalize, window=None):
    n_sub, n_maps = len(shifts), len(shifts[0])
    dacc = pv_stage.rows
    zero_acc = lambda: tuple(jnp.zeros((dacc, tq), F32) for _ in range(n_maps))

    def run_tasks(tasks, accs):
        queue = []
        for idx, (sub, c, j) in enumerate(tasks):
            while len(queue) + idx < min(idx + QK_LOOKAHEAD + 1, len(tasks)):
                ahead_sub, ahead_c, ahead_j = tasks[idx + len(queue)]
                queue.append(qk_stage(ahead_c, ahead_j, ahead_sub))
            tile, cst = queue.pop(0)
            p = jnp.exp2(tile - (shifts[sub][j] - cst)).astype(BF16)
            accs[sub][j] = accs[sub][j] + pv_stage(c, p)

    unroll = math.gcd(nchunks, KV_UNROLL)
    accs = [list(zero_acc()) for _ in range(n_sub)]
    if window is not None:
        n_win, first = window
        starts = [first(sub) for sub in range(n_sub)]
        run_tasks([(sub, starts[sub] + w, j) for sub in range(n_sub) for w in range(n_win)
                   for j in range(n_maps)], accs)
    elif unroll == nchunks:
        run_tasks([(sub, c, j) for sub in range(n_sub) for c in range(nchunks)
                   for j in range(n_maps)], accs)
    else:
        for sub in range(n_sub):
            def trip(t, acc, sub=sub):
                acc = {sub: list(acc)}
                run_tasks([(sub, t * unroll + u, j) for u in range(unroll) for j in range(n_maps)], acc)
                return tuple(acc[sub])
            accs[sub] = list(lax.fori_loop(0, nchunks // unroll, trip, tuple(accs[sub])))

    for sub in range(n_sub):
        finalize(sub, accs[sub])
        lmin = functools.reduce(jnp.minimum, [jnp.min(a[dacc - 16:dacc - 15]) for a in accs[sub]])

        @pl.when(jnp.logical_not(lmin >= SHIFT_OK_MIN_SUM))
        def _(sub=sub):
            def online_trip(c, carry):
                m, acc = carry
                out = []
                for j in range(n_maps):
                    tile, cst = qk_stage(c, j, sub)
                    m_new, alpha, p = _softmax_stage(tile, cst, m[j])
                    out.append((m_new, alpha * acc[j] + pv_stage(c, p)))
                return tuple(o[0] for o in out), tuple(o[1] for o in out)

            m0 = tuple(jnp.full((1, tq), NEG_BIG, F32) for _ in range(n_maps))
            finalize(sub, lax.fori_loop(0, nchunks, online_trip, (m0, zero_acc()))[1])


def _chunk_slice(c, tk):
    return pl.ds(c * tk if isinstance(c, int) else pl.multiple_of(c * tk, tk), tk)


def _make_pv_stage(v_ref, tk):
    dv = v_ref.shape[1]
    ones = jnp.ones((16, tk), BF16)

    def pv_stage(c, p):
        vaug = jnp.concatenate([v_ref[0, :, _chunk_slice(c, tk)], ones], axis=0)
        return _dot(vaug, p)

    pv_stage.rows = dv + 16
    return pv_stage


BOUND_SLACK = 1.01


def _score_bound(q32, kn2):
    return BOUND_SLACK * jnp.sqrt(jnp.sum(q32 * q32, axis=0, keepdims=True)
                                  * jnp.max(kn2, axis=1, keepdims=True))


def _attn_kernel(q_ref, k_ref, v_ref, kn_ref, o_ref, *, k_slots, nchunks, tk, tq):
    n_sub = q_ref.shape[2] // tq
    dv = v_ref.shape[1]
    qp, shifts = [], []
    for sub in range(n_sub):
        q32 = q_ref[0, :, sub * tq:(sub + 1) * tq].astype(F32)
        shifts.append((_score_bound(q32, kn_ref[0, 0]),))
        if k_slots > 1:
            q32 = _pad_rows(q32, pl.program_id(1) // 2, k_slots)
        qp.append(q32.astype(BF16))
    k_at = (lambda sl: k_ref[0, sl, :]) if len(k_ref.shape) == 3 else (lambda sl: k_ref[0, 0, sl, :])

    def qk_stage(c, j, sub):
        return _dot(k_at(_chunk_slice(c, tk)), qp[sub]), jnp.float32(0.0)

    def finalize(sub, acc):
        o_ref[0, :, sub * tq:(sub + 1) * tq] = acc[0][0:dv] / acc[0][dv:dv + 1]

    _attention(nchunks, tq, shifts, qk_stage, _make_pv_stage(v_ref, tk), finalize)


def _attn_a(qT, k, vT, kn):
    B, _, S = qT.shape
    tq, tk = min(Q_TILE, S), min(KV_CHUNK, S)
    tstep = tq * math.gcd(S // tq, Q_SUBTILES)
    return pl.pallas_call(
        functools.partial(_attn_kernel, k_slots=2, nchunks=S // tk, tk=tk, tq=tq),
        grid=(B, 4, S // tstep),
        in_specs=[pl.BlockSpec((1, 64, tstep), lambda b, h, i: (b, h, i)),
                  pl.BlockSpec((1, S, 128), lambda b, h, i: (b, 0, 0)),
                  pl.BlockSpec((1, 64, S), lambda b, h, i: (b, h // 2, 0)),
                  pl.BlockSpec((1, 1, 1, S), lambda b, h, i: (b, h // 2, 0, 0))],
        out_specs=pl.BlockSpec((1, 64, tstep), lambda b, h, i: (b, h, i)),
        out_shape=jax.ShapeDtypeStruct((B, 256, S), F32),
        compiler_params=pltpu.CompilerParams(vmem_limit_bytes=VMEM_LIMIT),
        name="attn_gqa",
    )(qT, k, vT, kn)


def _attn_b(qT, k, vT, kn):
    B, _, S = qT.shape
    tq, tk = min(Q_TILE, S), min(KV_CHUNK, S)
    tstep = tq * math.gcd(S // tq, Q_SUBTILES)
    return pl.pallas_call(
        functools.partial(_attn_kernel, k_slots=1, nchunks=S // tk, tk=tk, tq=tq),
        grid=(B, 4, S // tstep),
        in_specs=[pl.BlockSpec((1, 128, tstep), lambda b, h, i: (b, h, i)),
                  pl.BlockSpec((1, 1, S, 128), lambda b, h, i: (b, h, 0, 0)),
                  pl.BlockSpec((1, 64, S), lambda b, h, i: (b, h, 0)),
                  pl.BlockSpec((1, 1, 1, S), lambda b, h, i: (b, h, 0, 0))],
        out_specs=pl.BlockSpec((1, 64, tstep), lambda b, h, i: (b, h, i)),
        out_shape=jax.ShapeDtypeStruct((B, 256, S), F32),
        compiler_params=pltpu.CompilerParams(vmem_limit_bytes=VMEM_LIMIT),
        name="attn_mla",
    )(qT, k, vT, kn)


def _diff_kernel(slope_ref, lam_ref, q_ref, k_ref, v_ref, kn_ref, g_ref, o_ref, bias_ref,
                 *, nchunks, tk, tq, head0, window, out_scale, lambda_init):
    h = head0 + pl.program_id(1)
    i = pl.program_id(2)
    n_sub = q_ref.shape[2] // tq
    dv = v_ref.shape[1]
    n_inside = tq // tk
    slope = slope_ref[h]

    @pl.when(i == 0)
    def _():
        d = (lax.broadcasted_iota(jnp.int32, (tk, tq), 1)
             - lax.broadcasted_iota(jnp.int32, (tk, tq), 0)).astype(F32)
        bias_ref[0] = -d * slope
        bias_ref[1] = d * slope
        for j in range(n_inside):
            bias_ref[2 + j] = -jnp.abs(d - float(j * tk)) * slope

    slot0 = (h % 2) * 2
    qp, shifts = [], []
    for sub in range(n_sub):
        q32 = q_ref[0, :, sub * tq:(sub + 1) * tq].astype(F32)
        maps = [q32[32 * mp:32 * mp + 32] for mp in range(2)]
        qp.append([_pad_rows(maps[mp], slot0 + mp, 4).astype(BF16) for mp in range(2)])
        shifts.append(tuple(_score_bound(maps[mp], kn_ref[0, 0, mp:mp + 1, :]) for mp in range(2)))

    def qk_stage(c, mp, sub):
        kc = k_ref[0, 0, _chunk_slice(c, tk), :]
        gap = (i * n_sub + sub) * tq - c * tk
        inside = jnp.logical_and(gap <= 0, gap > -tq)
        bias = bias_ref[jnp.where(gap > 0, 0, jnp.where(inside, 2 + (-gap) // tk, 1))]
        cst = jnp.where(inside, 0.0, -jnp.abs(gap).astype(F32) * slope)
        return _dot(kc, qp[sub][mp]) + bias, cst

    def finalize(sub, acc):
        a1, a2 = acc
        lv = lam_ref[...]
        lam = (jnp.exp(jnp.sum(lv[0:1] * lv[1:2], axis=1, keepdims=True))
               - jnp.exp(jnp.sum(lv[2:3] * lv[3:4], axis=1, keepdims=True)) + lambda_init)
        o = a1[0:dv] / a1[dv:dv + 1] - lam * (a2[0:dv] / a2[dv:dv + 1])
        o_ref[0, :, sub * tq:(sub + 1) * tq] = _rms_rows(o, g_ref[...]) * out_scale

    chunk_window = None
    if window is not None:
        n_win, reach = window
        first = lambda sub: jnp.clip(((i * n_sub + sub) * tq - reach) // tk, 0, nchunks - n_win)
        chunk_window = (n_win, first)
    _attention(nchunks, tq, shifts, qk_stage, _make_pv_stage(v_ref, tk), finalize, chunk_window)


def _alibi_window(slope2, n_tokens, tq, tk):
    reach = math.ceil(ZERO_PROB_LOG2 / slope2)
    n_win = -(-(tq + 2 * reach) // tk) + 1
    nchunks = n_tokens // tk
    if n_win >= nchunks:
        return None
    for q0 in range(0, n_tokens, tq):
        first = min(max((q0 - reach) // tk, 0), nchunks - n_win)
        lo, hi = max(q0 - reach, 0) // tk, min(q0 + tq - 1 + reach, n_tokens - 1) // tk
        assert first <= lo and hi < first + n_win
    return n_win, reach


def _attn_c(qT, k, vT, kn, slopes2, lam_vecs, g_head, lambda_init, heads, window):
    B, _, S = qT.shape
    tq, tk = min(Q_TILE, S), min(KV_CHUNK, S)
    tstep = tq * math.gcd(S // tq, Q_SUBTILES)
    smem = pl.BlockSpec(memory_space=pltpu.SMEM)
    h0 = heads[0]
    return pl.pallas_call(
        functools.partial(_diff_kernel, nchunks=S // tk, tk=tk, tq=tq, head0=h0, window=window,
                          out_scale=1.0 - lambda_init, lambda_init=lambda_init),
        grid=(B, len(heads), S // tstep),
        in_specs=[smem,
                  pl.BlockSpec((4, 32), lambda b, h, i: (0, 0)),
                  pl.BlockSpec((1, 64, tstep), lambda b, h, i: (b, h0 + h, i)),
                  pl.BlockSpec((1, 1, S, 128), lambda b, h, i: (b, (h0 + h) // 2, 0, 0)),
                  pl.BlockSpec((1, 64, S), lambda b, h, i: (b, h0 + h, 0)),
                  pl.BlockSpec((1, 1, 2, S), lambda b, h, i: (b, h0 + h, 0, 0)),
                  pl.BlockSpec((64, 1), lambda b, h, i: (0, 0))],
        out_specs=pl.BlockSpec((1, 64, tstep), lambda b, h, i: (b, h, i)),
        out_shape=jax.ShapeDtypeStruct((B, 64 * len(heads), S), F32),
        scratch_shapes=[pltpu.VMEM((2 + tq // tk, tk, tq), F32)],
        compiler_params=pltpu.CompilerParams(vmem_limit_bytes=VMEM_LIMIT),
        name="attn_diff",
    )(slopes2, lam_vecs, qT, k, vT, kn, g_head)


def _na_kernel(bmax_ref, q_ref, k_ref, v_ref, kn_ref, bias_ref, o_ref, *, n_rows):
    h = pl.program_id(0)
    n_blocks = n_rows // 4
    nk = NA_SLAB_ROWS * GRID_W
    tq = NA_Q_TILE
    dv = v_ref.shape[1]
    ones = jnp.ones((16, nk), BF16)
    kmax2 = jnp.max(kn_ref[0, 0], axis=1, keepdims=True)
    bmax = bmax_ref[h]

    def variant(a):
        return 0 if a == 0 else (2 if a == n_blocks - 1 else 1)

    def slab(a):
        return min(max(4 * a - 4, 0), n_rows - NA_SLAB_ROWS) * GRID_W

    def q_block(a):
        return q_ref[0, :, a * tq:(a + 1) * tq].astype(F32)

    def scores(a):
        qp = _pad_rows(q_block(a), h % 2, 2).astype(BF16)
        return _dot(k_ref[0, 0, slab(a):slab(a) + nk, :], qp) + bias_ref[0, variant(a)]

    def pv(a, p):
        vaug = jnp.concatenate([v_ref[0, :, slab(a):slab(a) + nk], ones], axis=0)
        return _dot(vaug, p)

    lmin = None
    nxt = scores(0)
    for a in range(n_blocks):
        s = nxt
        if a + 1 < n_blocks:
            nxt = scores(a + 1)
        q32 = q_block(a)
        shift = jnp.sqrt(jnp.sum(q32 * q32, axis=0, keepdims=True) * kmax2) + bmax
        acc = pv(a, jnp.exp2(s - shift).astype(BF16))
        o_ref[0, :, a * tq:(a + 1) * tq] = acc[0:dv] / acc[dv:dv + 1]
        amin = jnp.min(acc[dv:dv + 1])
        lmin = amin if lmin is None else jnp.minimum(lmin, amin)

    @pl.when(jnp.logical_not(lmin >= SHIFT_OK_MIN_SUM))
    def _():
        for a in range(n_blocks):
            s = scores(a)
            p = jnp.exp2(s - jnp.max(s, axis=0, keepdims=True)).astype(BF16)
            acc = pv(a, p)
            o_ref[0, :, a * tq:(a + 1) * tq] = acc[0:dv] / acc[dv:dv + 1]


def _na_bias_tables(rel_bias, n_rows):
    n_blocks = n_rows // 4
    kr_w = min(NA_ROWS, n_rows)
    col = np.arange(GRID_W)
    c0 = np.clip(col - NA_COLS // 2, 0, GRID_W - NA_COLS)
    col_ok = (col[:, None] >= c0[None, :]) & (col[:, None] < c0[None, :] + NA_COLS)
    dc = col[:, None] - col[None, :] + (NA_COLS - 1)
    onehot_c = ((dc[None] == np.arange(2 * NA_COLS - 1)[:, None, None]) & col_ok[None]).astype(np.float32)

    def row_geometry(a):
        R0 = int(np.clip(4 * a - 4, 0, n_rows - NA_SLAB_ROWS))
        r = 4 * a + np.arange(4)
        r0 = np.clip(r - kr_w // 2, 0, n_rows - kr_w)
        kr = R0 + np.arange(NA_SLAB_ROWS)
        ok = (kr[:, None] >= r0[None, :]) & (kr[:, None] < r0[None, :] + kr_w)
        dr = kr[:, None] - r[None, :] + (NA_ROWS - 1)
        return ((dr[:, :, None] == np.arange(2 * NA_ROWS - 1)) & ok[:, :, None]), ok

    variants = [row_geometry(0), row_geometry(1), row_geometry(n_blocks - 1)]
    for a in range(1, n_blocks - 1):
        assert np.array_equal(row_geometry(a)[0], variants[1][0])
    onehot_r = np.stack([v[0] for v in variants]).astype(np.float32)
    valid = (np.stack([v[1] for v in variants])[:, :, None, :, None]
             & col_ok[None, None, :, None, :])
    assert (valid.reshape(3, -1, NA_Q_TILE).sum(axis=1) == kr_w * NA_COLS).all()
    hi = lax.Precision.HIGHEST
    t1 = jnp.einsum('hrd,dkc->hrkc', rel_bias, onehot_c, precision=hi)
    tab = jnp.einsum('viqr,hrkc->hvikqc', onehot_r, t1, precision=hi) * LOG2E
    tab = jnp.where(valid[None], tab, NEG_BIG)
    return tab.reshape(4, 3, NA_SLAB_ROWS * GRID_W, NA_Q_TILE).astype(F32)


def _attn_d(qT, k, vT, kn, bias_tab, bias_max):
    B, _, S = qT.shape
    n_rows = S // GRID_W
    nk = NA_SLAB_ROWS * GRID_W
    return pl.pallas_call(
        functools.partial(_na_kernel, n_rows=n_rows),
        grid=(4, B),
        in_specs=[pl.BlockSpec(memory_space=pltpu.SMEM),
                  pl.BlockSpec((1, 64, S), lambda h, b: (b, h, 0)),
                  pl.BlockSpec((1, 1, S, 128), lambda h, b: (b, h // 2, 0, 0)),
                  pl.BlockSpec((1, 64, S), lambda h, b: (b, h, 0)),
                  pl.BlockSpec((1, 1, 1, S), lambda h, b: (b, h, 0, 0)),
                  pl.BlockSpec((1, 3, nk, NA_Q_TILE), lambda h, b: (h, 0, 0, 0))],
        out_specs=pl.BlockSpec((1, 64, S), lambda h, b: (b, h, 0)),
        out_shape=jax.ShapeDtypeStruct((B, 256, S), F32),
        compiler_params=pltpu.CompilerParams(vmem_limit_bytes=VMEM_LIMIT),
        name="attn_nbr",
    )(bias_max, qT, k, vT, kn, bias_tab)


def _post_kernel(x_ref, oa_ref, ob_ref, od_ref, *rest, ff_chunk, n_parts, n_oc):
    oc_refs = rest[:n_oc]
    (ga_ref, gb_ref, gd_ref, wout_ref, gpost_ref, gmlp_ref, wup_ref, wdown_ref, gmlp_post_ref,
     y_ref) = rest[n_oc:]
    tm = x_ref.shape[2]
    parts = [slice(s * tm // n_parts, (s + 1) * tm // n_parts) for s in range(n_parts)]
    n_ff = wup_ref.shape[0] // ff_chunk

    def mixer_out(cols):
        cat = jnp.concatenate([
            _rms_rows(oa_ref[0, :, cols], ga_ref[...]).astype(BF16),
            _rms_rows(ob_ref[0, :, cols], gb_ref[...]).astype(BF16),
            *[oc_ref[0, :, cols].astype(BF16) for oc_ref in oc_refs],
            _rms_rows(od_ref[0, :, cols], gd_ref[...]).astype(BF16)], axis=0)
        return _dot(wout_ref[...], cat)

    def mlp(h):
        f = None
        for j in range(n_ff):
            up = _dot(wup_ref[j * ff_chunk:(j + 1) * ff_chunk, :], h)
            act = jnp.square(jnp.maximum(up, 0.0)).astype(BF16)
            down = _dot(wdown_ref[:, j * ff_chunk:(j + 1) * ff_chunk], act)
            f = down if f is None else f + down
        return f

    mix = [mixer_out(cols) for cols in parts]
    x1, f = [], []
    for s, cols in enumerate(parts):
        x1.append(x_ref[0, :, cols] + _rms_rows(mix[s], gpost_ref[...]))
        f.append(mlp(_rms_rows(x1[s], gmlp_ref[...]).astype(BF16)))
    for s, cols in enumerate(parts):
        y_ref[0, :, cols] = x1[s] + _rms_rows(f[s], gmlp_post_ref[...])


def _post(xT, oa, ob, oc_parts, od, ga, gb, gd, woutT, gpost, gmlp, wupT, wdownT, gmlp_post):
    B, D, S = xT.shape
    tm = min(MLP_TOKEN_TILE, S)
    rows_spec = lambda rows: pl.BlockSpec((1, rows, tm), lambda b, i: (b, 0, i))
    consts = (ga, gb, gd, woutT, gpost, gmlp, wupT, wdownT, gmlp_post)
    return pl.pallas_call(
        functools.partial(_post_kernel, ff_chunk=1024, n_parts=tm // math.gcd(tm, MLP_PART),
                          n_oc=len(oc_parts)),
        grid=(B, S // tm),
        in_specs=([rows_spec(D)] + [rows_spec(256)] * 3 + [rows_spec(o.shape[1]) for o in oc_parts]
                  + [_const_spec(c.shape) for c in consts]),
        out_specs=rows_spec(D),
        out_shape=jax.ShapeDtypeStruct((B, D, S), F32),
        compiler_params=pltpu.CompilerParams(vmem_limit_bytes=VMEM_LIMIT),
        name="outproj_mlp",
    )(xT, oa, ob, od, *oc_parts, *consts)


def _rope_tables(n_tokens):
    def angles(pos, dim):
        inv = ROPE_BASE ** (-jnp.arange(0, dim, 2, dtype=F32) / dim)
        return pos.astype(F32)[:, None] * inv[None, :]

    t = jnp.arange(n_tokens, dtype=jnp.int32)
    out = []
    for ang in (angles(t // GRID_W, HEAD_DIM // 2), angles(t % GRID_W, HEAD_DIM // 2),
                angles(t, MLA_ROPE)):
        out += [jnp.cos(ang).T, jnp.sin(ang).T]
    return out


def kernel(x, norm_mix_pre, norm_mix_post, norm_mlp_pre, norm_mlp_post, w_in, a_q_norm, a_k_norm,
           b_cq_norm, b_ckv_norm, b_w_uq, b_w_ukv, c_lambda_q1, c_lambda_k1, c_lambda_q2,
           c_lambda_k2, d_rel_bias, g_out_a, g_out_b, g_out_c, g_out_d, w_out, w_up, w_down):
    B, S, D = x.shape
    depth = w_in.shape[0]
    col = lambda v: v.astype(F32).reshape(-1, 1)
    wt = lambda w: w.T.astype(BF16)
    tables = _rope_tables(S)
    slopes = [2.0 ** (-8.0 * (hd + 1) / 4) for hd in range(4)]
    slopes2 = jnp.asarray(slopes, F32) * LOG2E
    tq, tk = min(Q_TILE, S), min(KV_CHUNK, S)
    windows = [_alibi_window(sl * LOG2E, S, tq, tk) for sl in slopes]
    head_groups = []
    for hd in range(4):
        if head_groups and windows[hd] is None and head_groups[-1][1] is None:
            head_groups[-1][0].append(hd)
        else:
            head_groups.append(([hd], windows[hd]))
    xT = jnp.transpose(x, (0, 2, 1))
    for l in range(depth):
        (qa, ka, va, qb, kb, vb, qc, kc, vc, qd, kd, vd, kna, knb, knc, knd) = _inproj(
            xT, col(norm_mix_pre[l]), wt(w_in[l]), col(a_q_norm[l]), col(a_k_norm[l]),
            col(b_cq_norm[l]), col(b_ckv_norm[l]), wt(b_w_uq[l]), wt(b_w_ukv[l]), tables)
        lambda_init = 0.8 - 0.6 * math.exp(-0.3 * l)
        lam_vecs = jnp.stack([c_lambda_q1[l], c_lambda_k1[l], c_lambda_q2[l], c_lambda_k2[l]]).astype(F32)
        oa = _attn_a(qa, ka, va, kna)
        ob = _attn_b(qb, kb, vb, knb)
        oc = [_attn_c(qc, kc, vc, knc, slopes2, lam_vecs, col(g_out_c[l]), lambda_init, tuple(hs), win)
              for hs, win in head_groups]
        rel_bias = d_rel_bias[l].astype(F32)
        od = _attn_d(qd, kd, vd, knd, _na_bias_tables(rel_bias, S // GRID_W),
                     jnp.max(rel_bias, axis=(1, 2)) * LOG2E)
        xT = _post(xT, oa, ob, oc, od, col(g_out_a[l]), col(g_out_b[l]), col(g_out_d[l]),
                   wt(w_out[l]), col(norm_mix_post[l]), col(norm_mlp_pre[l]), wt(w_up[l]),
                   wt(w_down[l]), col(norm_mlp_post[l]))
    return jnp.transpose(xT, (0, 2, 1))
```

```python
import functools
import math

import numpy as np
import jax
import jax.numpy as jnp
from jax import lax
from jax.experimental import pallas as pl
from jax.experimental.pallas import tpu as pltpu

GRID_W = 64
HEAD_DIM = 64
GROUP_W = 256
N_KV_A = 2
MLA_Q_RANK = 192
MLA_KV_RANK = 128
MLA_NOPE = 64
MLA_ROPE = 32
DIFF_QK = 32
NA_ROWS = 8
NA_COLS = 16
ROPE_BASE = 10000.0
EPS = 1e-6
LOG2E = math.log2(math.e)
NEG_BIG = -1e30

A_COLS = 512
B_COLS = MLA_Q_RANK + MLA_KV_RANK + MLA_ROPE
C_COLS = 768
D_COLS = 768
OFF_B = A_COLS
OFF_C = OFF_B + B_COLS
OFF_D = OFF_C + C_COLS
IN_COLS = OFF_D + D_COLS

TOKEN_TILE = 512
MLP_TOKEN_TILE = 512
MLP_PART = 512
Q_TILE = 512
Q_SUBTILES = 4
KV_CHUNK = 256
QK_LOOKAHEAD = 2
KV_UNROLL = 16
NA_Q_TILE = 256
NA_SLAB_ROWS = 12
VMEM_LIMIT = 56 * 1024 * 1024

F32 = jnp.float32
BF16 = jnp.bfloat16
LANES = 128


def _dot(a, b):
    return jnp.dot(a, b, preferred_element_type=F32)


def _rms_rows(x, gain):
    gain = jnp.concatenate([gain] * (x.shape[1] // LANES), axis=1)
    ms = jnp.mean(x * x, axis=0, keepdims=True)
    return x * lax.rsqrt(ms + EPS) * gain


def _rope_rows(x, cos, sin):
    n = cos.shape[0]
    x1, x2 = x[:n], x[n:]
    return x1 * cos - x2 * sin, x1 * sin + x2 * cos


def _inproj_kernel(x_ref, gpre_ref, w_ref, aq_ref, ak_ref, cqg_ref, ckvg_ref, wuq_ref, wukv_ref,
                   cr_ref, sr_ref, cc_ref, sc_ref, cs_ref, ss_ref,
                   qa_ref, ka_ref, va_ref, qb_ref, kb_ref, vb_ref,
                   qc_ref, kc_ref, vc_ref, qd_ref, kd_ref, vd_ref,
                   kna_ref, knb_ref, knc_ref, knd_ref, *xt_ref):
    if xt_ref:
        x = x_ref[0].T
        xt_ref[0][0] = x
    else:
        x = x_ref[0]
    h = _rms_rows(x, gpre_ref[...]).astype(BF16)
    cr, sr, cc, sc = cr_ref[...], sr_ref[...], cc_ref[...], sc_ref[...]
    cs, ss = cs_ref[...], ss_ref[...]
    tm = x.shape[1]

    def axial(t):
        r1, r2 = _rope_rows(t[0:32], cr, sr)
        c1, c2 = _rope_rows(t[32:64], cc, sc)
        return r1, r2, c1, c2

    def sumsq(t):
        return jnp.sum(t * t, axis=0, keepdims=True)

    ua = _dot(w_ref[0:A_COLS, :], h)
    qscale = HEAD_DIM ** -0.5 * LOG2E
    for hd in range(4):
        t = _rms_rows(ua[64 * hd:64 * hd + 64], aq_ref[...])
        for j, piece in enumerate(axial(t)):
            qa_ref[0, 64 * hd + 16 * j:64 * hd + 16 * j + 16, :] = (piece * qscale).astype(BF16)
    kparts = []
    for g in range(N_KV_A):
        t = _rms_rows(ua[256 + 64 * g:256 + 64 * g + 64], ak_ref[...])
        kparts.extend(axial(t))
        kna_ref[0, g] = sumsq(t)
    ka_ref[0] = jnp.concatenate(kparts, axis=0).T.astype(BF16)
    va_ref[0] = ua[384:512].astype(BF16)

    ub = _dot(w_ref[OFF_B:OFF_B + B_COLS, :], h)
    cqn = _rms_rows(ub[0:MLA_Q_RANK], cqg_ref[...]).astype(BF16)
    ckvn = _rms_rows(ub[MLA_Q_RANK:MLA_Q_RANK + MLA_KV_RANK], ckvg_ref[...]).astype(BF16)
    kr1, kr2 = _rope_rows(ub[MLA_Q_RANK + MLA_KV_RANK:B_COLS], cs, ss)
    qb = _dot(wuq_ref[...], cqn)
    kvb = _dot(wukv_ref[...], ckvn)
    bscale = (MLA_NOPE + MLA_ROPE) ** -0.5 * LOG2E
    zeros32 = jnp.zeros((32, tm), F32)
    for hd in range(4):
        base = 96 * hd
        q1, q2 = _rope_rows(qb[base + 64:base + 96], cs, ss)
        qb_ref[0, 128 * hd:128 * hd + 64, :] = (qb[base:base + 64] * bscale).astype(BF16)
        qb_ref[0, 128 * hd + 64:128 * hd + 80, :] = (q1 * bscale).astype(BF16)
        qb_ref[0, 128 * hd + 80:128 * hd + 96, :] = (q2 * bscale).astype(BF16)
        qb_ref[0, 128 * hd + 96:128 * hd + 128, :] = jnp.zeros((32, tm), BF16)
        kfull = jnp.concatenate([kvb[128 * hd:128 * hd + 64], kr1, kr2, zeros32], axis=0)
        kb_ref[0, hd] = kfull.T.astype(BF16)
        knb_ref[0, hd] = sumsq(kfull)
        vb_ref[0, 64 * hd:64 * hd + 64, :] = kvb[128 * hd + 64:128 * hd + 128].astype(BF16)

    uc = _dot(w_ref[OFF_C:OFF_C + C_COLS, :], h)
    qc_ref[0] = (uc[0:256] * (DIFF_QK ** -0.5 * LOG2E)).astype(BF16)
    for p in range(2):
        kc_ref[0, p] = uc[256 + 128 * p:256 + 128 * p + 128].T.astype(BF16)
    vc_ref[0] = uc[512:768].astype(BF16)
    for hd in range(4):
        for mp in range(2):
            base = 256 + 64 * hd + 32 * mp
            knc_ref[0, hd, mp:mp + 1, :] = sumsq(uc[base:base + 32])

    ud = _dot(w_ref[OFF_D:OFF_D + D_COLS, :], h)
    qd_ref[0] = (ud[0:256] * (HEAD_DIM ** -0.5 * LOG2E)).astype(BF16)
    for p in range(2):
        kd_ref[0, p] = ud[256 + 128 * p:256 + 128 * p + 128].T.astype(BF16)
    vd_ref[0] = ud[512:768].astype(BF16)
    for hd in range(4):
        knd_ref[0, hd] = sumsq(ud[256 + 64 * hd:256 + 64 * hd + 64])


def _const_spec(shape):
    nd = len(shape)
    return pl.BlockSpec(shape, lambda *_: (0,) * nd, pipeline_mode=pl.Buffered(1))


def _inproj(x, gpre, wT, aq, ak, cqg, ckvg, wuqT, wukvT, tables, token_major):
    B, D, S = (x.shape[0], x.shape[2], x.shape[1]) if token_major else x.shape
    tm = min(TOKEN_TILE, S)
    grid = (B, S // tm)
    tab_spec = pl.BlockSpec((16, tm), lambda b, i: (0, i))
    rows_spec = lambda rows: pl.BlockSpec((1, rows, tm), lambda b, i: (b, 0, i))
    out_shape = (
        jax.ShapeDtypeStruct((B, 256, S), BF16),
        jax.ShapeDtypeStruct((B, S, 128), BF16),
        jax.ShapeDtypeStruct((B, 128, S), BF16),
        jax.ShapeDtypeStruct((B, 512, S), BF16),
        jax.ShapeDtypeStruct((B, 4, S, 128), BF16),
        jax.ShapeDtypeStruct((B, 256, S), BF16),
        jax.ShapeDtypeStruct((B, 256, S), BF16),
        jax.ShapeDtypeStruct((B, 2, S, 128), BF16),
        jax.ShapeDtypeStruct((B, 256, S), BF16),
        jax.ShapeDtypeStruct((B, 256, S), BF16),
        jax.ShapeDtypeStruct((B, 2, S, 128), BF16),
        jax.ShapeDtypeStruct((B, 256, S), BF16),
        jax.ShapeDtypeStruct((B, 2, 1, S), F32),
        jax.ShapeDtypeStruct((B, 4, 1, S), F32),
        jax.ShapeDtypeStruct((B, 4, 2, S), F32),
        jax.ShapeDtypeStruct((B, 4, 1, S), F32),
    )
    kspec = lambda n: pl.BlockSpec((1, n, tm, 128), lambda b, i: (b, 0, i, 0))
    nspec = lambda n, r: pl.BlockSpec((1, n, r, tm), lambda b, i: (b, 0, 0, i))
    out_specs = (
        rows_spec(256), pl.BlockSpec((1, tm, 128), lambda b, i: (b, i, 0)), rows_spec(128),
        rows_spec(512), kspec(4), rows_spec(256),
        rows_spec(256), kspec(2), rows_spec(256),
        rows_spec(256), kspec(2), rows_spec(256),
        nspec(2, 1), nspec(4, 1), nspec(4, 2), nspec(4, 1),
    )
    x_spec = rows_spec(D)
    if token_major:
        x_spec = pl.BlockSpec((1, tm, D), lambda b, i: (b, i, 0))
        out_shape += (jax.ShapeDtypeStruct((B, D, S), F32),)
        out_specs += (rows_spec(D),)
    in_specs = [
        x_spec, _const_spec(gpre.shape), _const_spec(wT.shape),
        _const_spec(aq.shape), _const_spec(ak.shape), _const_spec(cqg.shape), _const_spec(ckvg.shape),
        _const_spec(wuqT.shape), _const_spec(wukvT.shape),
    ] + [tab_spec] * 6
    return pl.pallas_call(
        _inproj_kernel, grid=grid, in_specs=in_specs, out_specs=out_specs, out_shape=out_shape,
        compiler_params=pltpu.CompilerParams(vmem_limit_bytes=VMEM_LIMIT),
        name="inproj",
    )(x, gpre, wT, aq, ak, cqg, ckvg, wuqT, wukvT, *tables)


def _pad_rows(q32, slot, nslots):
    return jnp.concatenate([jnp.where(slot == j, q32, 0.0) for j in range(nslots)], axis=0)


def _softmax_stage(t, cst, m):
    m_new = jnp.maximum(m, jnp.max(t, axis=0, keepdims=True) + cst)
    alpha = jnp.exp2(m - m_new)
    p = jnp.exp2(t - (m_new - cst))
    return m_new, alpha, p


def _fold_rows(p):
    slabs = [p[r:r + 8] for r in range(0, p.shape[0], 8)]
    chains = [functools.reduce(jnp.add, slabs[w::4]) for w in range(min(4, len(slabs)))]
    return functools.reduce(jnp.add, chains)


SHIFT_OK_MIN_SUM = 2.0 ** -64
ZERO_PROB_LOG2 = 140.0


def _attention(nchunks, tq, shifts, qk_stage, pv_stage, finalize, window=None):
    n_sub, n_maps = len(shifts), len(shifts[0])
    dv = pv_stage.rows
    zero_acc = lambda: tuple(jnp.zeros((dv, tq), F32) for _ in range(n_maps))
    zero_sum = lambda: tuple(jnp.zeros((8, tq), F32) for _ in range(n_maps))

    def run_tasks(tasks, accs, sums):
        queue = []
        for idx, (sub, c, j) in enumerate(tasks):
            while len(queue) + idx < min(idx + QK_LOOKAHEAD + 1, len(tasks)):
                ahead_sub, ahead_c, ahead_j = tasks[idx + len(queue)]
                queue.append(qk_stage(ahead_c, ahead_j, ahead_sub))
            tile, cst = queue.pop(0)
            p = jnp.exp2(tile - (shifts[sub][j] - cst))
            sums[sub][j] = sums[sub][j] + _fold_rows(p)
            accs[sub][j] = accs[sub][j] + pv_stage(c, p.astype(BF16))

    unroll = math.gcd(nchunks, KV_UNROLL)
    accs = [list(zero_acc()) for _ in range(n_sub)]
    sums = [list(zero_sum()) for _ in range(n_sub)]
    if window is not None:
        n_win, first = window
        starts = [first(sub) for sub in range(n_sub)]
        run_tasks([(sub, starts[sub] + w, j) for sub in range(n_sub) for w in range(n_win)
                   for j in range(n_maps)], accs, sums)
    elif unroll == nchunks:
        run_tasks([(sub, c, j) for sub in range(n_sub) for c in range(nchunks)
                   for j in range(n_maps)], accs, sums)
    else:
        for sub in range(n_sub):
            def trip(t, carry, sub=sub):
                acc, ssum = {sub: list(carry[0])}, {sub: list(carry[1])}
                run_tasks([(sub, t * unroll + u, j) for u in range(unroll) for j in range(n_maps)],
                          acc, ssum)
                return tuple(acc[sub]), tuple(ssum[sub])
            done = lax.fori_loop(0, nchunks // unroll, trip, (tuple(accs[sub]), tuple(sums[sub])))
            accs[sub], sums[sub] = list(done[0]), list(done[1])

    for sub in range(n_sub):
        denom = [jnp.sum(s8, axis=0, keepdims=True) for s8 in sums[sub]]
        finalize(sub, list(zip(accs[sub], denom)))
        lmin = functools.reduce(jnp.minimum, [jnp.min(d) for d in denom])

        @pl.when(jnp.logical_not(lmin >= SHIFT_OK_MIN_SUM))
        def _(sub=sub):
            def online_trip(c, carry):
                out = []
                for j, (m, l, acc) in enumerate(carry):
                    tile, cst = qk_stage(c, j, sub)
                    m_new, alpha, p = _softmax_stage(tile, cst, m)
                    out.append((m_new, alpha * l + jnp.sum(p, axis=0, keepdims=True),
                                alpha * acc + pv_stage(c, p.astype(BF16))))
                return tuple(out)

            init = tuple((jnp.full((1, tq), NEG_BIG, F32), jnp.zeros((1, tq), F32), acc0)
                         for acc0 in zero_acc())
            done = lax.fori_loop(0, nchunks, online_trip, init)
            finalize(sub, [(acc, l) for _, l, acc in done])


def _chunk_slice(c, tk):
    return pl.ds(c * tk if isinstance(c, int) else pl.multiple_of(c * tk, tk), tk)


def _make_pv_stage(v_ref, tk):
    def pv_stage(c, p):
        return _dot(v_ref[0, :, _chunk_slice(c, tk)], p)

    pv_stage.rows = v_ref.shape[1]
    return pv_stage


BOUND_SLACK = 1.01


def _score_bound(q32, kn2):
    return BOUND_SLACK * jnp.sqrt(jnp.sum(q32 * q32, axis=0, keepdims=True)
                                  * jnp.max(kn2, axis=1, keepdims=True))


def _attn_kernel(q_ref, k_ref, v_ref, kn_ref, o_ref, *, k_slots, nchunks, tk, tq):
    n_sub = q_ref.shape[2] // tq
    qp, shifts = [], []
    for sub in range(n_sub):
        q32 = q_ref[0, :, sub * tq:(sub + 1) * tq].astype(F32)
        shifts.append((_score_bound(q32, kn_ref[0, 0]),))
        if k_slots > 1:
            q32 = _pad_rows(q32, pl.program_id(1) // 2, k_slots)
        qp.append(q32.astype(BF16))
    k_at = (lambda sl: k_ref[0, sl, :]) if len(k_ref.shape) == 3 else (lambda sl: k_ref[0, 0, sl, :])

    def qk_stage(c, j, sub):
        return _dot(k_at(_chunk_slice(c, tk)), qp[sub]), jnp.float32(0.0)

    def finalize(sub, acc):
        (o, l), = acc
        o_ref[0, :, sub * tq:(sub + 1) * tq] = o / l

    _attention(nchunks, tq, shifts, qk_stage, _make_pv_stage(v_ref, tk), finalize)


def _attn_a(qT, k, vT, kn):
    B, _, S = qT.shape
    tq, tk = min(Q_TILE, S), min(KV_CHUNK, S)
    tstep = tq * math.gcd(S // tq, Q_SUBTILES)
    return pl.pallas_call(
        functools.partial(_attn_kernel, k_slots=2, nchunks=S // tk, tk=tk, tq=tq),
        grid=(B, 4, S // tstep),
        in_specs=[pl.BlockSpec((1, 64, tstep), lambda b, h, i: (b, h, i)),
                  pl.BlockSpec((1, S, 128), lambda b, h, i: (b, 0, 0)),
                  pl.BlockSpec((1, 64, S), lambda b, h, i: (b, h // 2, 0)),
                  pl.BlockSpec((1, 1, 1, S), lambda b, h, i: (b, h // 2, 0, 0))],
        out_specs=pl.BlockSpec((1, 64, tstep), lambda b, h, i: (b, h, i)),
        out_shape=jax.ShapeDtypeStruct((B, 256, S), F32),
        compiler_params=pltpu.CompilerParams(vmem_limit_bytes=VMEM_LIMIT),
        name="attn_gqa",
    )(qT, k, vT, kn)


def _attn_b(qT, k, vT, kn):
    B, _, S = qT.shape
    tq, tk = min(Q_TILE, S), min(KV_CHUNK, S)
    tstep = tq * math.gcd(S // tq, Q_SUBTILES)
    return pl.pallas_call(
        functools.partial(_attn_kernel, k_slots=1, nchunks=S // tk, tk=tk, tq=tq),
        grid=(B, 4, S // tstep),
        in_specs=[pl.BlockSpec((1, 128, tstep), lambda b, h, i: (b, h, i)),
                  pl.BlockSpec((1, 1, S, 128), lambda b, h, i: (b, h, 0, 0)),
                  pl.BlockSpec((1, 64, S), lambda b, h, i: (b, h, 0)),
                  pl.BlockSpec((1, 1, 1, S), lambda b, h, i: (b, h, 0, 0))],
        out_specs=pl.BlockSpec((1, 64, tstep), lambda b, h, i: (b, h, i)),
        out_shape=jax.ShapeDtypeStruct((B, 256, S), F32),
        compiler_params=pltpu.CompilerParams(vmem_limit_bytes=VMEM_LIMIT),
        name="attn_mla",
    )(qT, k, vT, kn)


def _diff_kernel(slope_ref, lam_ref, q_ref, k_ref, v_ref, kn_ref, g_ref, o_ref, bias_ref,
                 *, nchunks, tk, tq, head0, window, out_scale, lambda_init):
    h = head0 + pl.program_id(1)
    i = pl.program_id(2)
    n_sub = q_ref.shape[2] // tq
    dv = v_ref.shape[1]
    n_inside = tq // tk
    slope = slope_ref[h]

    @pl.when(i == 0)
    def _():
        d = (lax.broadcasted_iota(jnp.int32, (tk, tq), 1)
             - lax.broadcasted_iota(jnp.int32, (tk, tq), 0)).astype(F32)
        bias_ref[0] = -d * slope
        bias_ref[1] = d * slope
        for j in range(n_inside):
            bias_ref[2 + j] = -jnp.abs(d - float(j * tk)) * slope

    slot0 = (h % 2) * 2
    qp, shifts = [], []
    for sub in range(n_sub):
        q32 = q_ref[0, :, sub * tq:(sub + 1) * tq].astype(F32)
        maps = [q32[32 * mp:32 * mp + 32] for mp in range(2)]
        qp.append([_pad_rows(maps[mp], slot0 + mp, 4).astype(BF16) for mp in range(2)])
        shifts.append(tuple(_score_bound(maps[mp], kn_ref[0, 0, mp:mp + 1, :]) for mp in range(2)))

    def qk_stage(c, mp, sub):
        kc = k_ref[0, 0, _chunk_slice(c, tk), :]
        gap = (i * n_sub + sub) * tq - c * tk
        inside = jnp.logical_and(gap <= 0, gap > -tq)
        bias = bias_ref[jnp.where(gap > 0, 0, jnp.where(inside, 2 + (-gap) // tk, 1))]
        cst = jnp.where(inside, 0.0, -jnp.abs(gap).astype(F32) * slope)
        return _dot(kc, qp[sub][mp]) + bias, cst

    def finalize(sub, acc):
        (a1, l1), (a2, l2) = acc
        lv = lam_ref[...]
        lam = (jnp.exp(jnp.sum(lv[0:1] * lv[1:2], axis=1, keepdims=True))
               - jnp.exp(jnp.sum(lv[2:3] * lv[3:4], axis=1, keepdims=True)) + lambda_init)
        o = a1 / l1 - lam * (a2 / l2)
        o_ref[0, :, sub * tq:(sub + 1) * tq] = _rms_rows(o, g_ref[...]) * out_scale

    chunk_window = None
    if window is not None:
        n_win, reach = window
        first = lambda sub: jnp.clip(((i * n_sub + sub) * tq - reach) // tk, 0, nchunks - n_win)
        chunk_window = (n_win, first)
    _attention(nchunks, tq, shifts, qk_stage, _make_pv_stage(v_ref, tk), finalize, chunk_window)


def _alibi_window(slope2, n_tokens, tq, tk):
    reach = math.ceil(ZERO_PROB_LOG2 / slope2)
    n_win = -(-(tq + 2 * reach) // tk) + 1
    nchunks = n_tokens // tk
    if n_win >= nchunks:
        return None
    for q0 in range(0, n_tokens, tq):
        first = min(max((q0 - reach) // tk, 0), nchunks - n_win)
        lo, hi = max(q0 - reach, 0) // tk, min(q0 + tq - 1 + reach, n_tokens - 1) // tk
        assert first <= lo and hi < first + n_win
    return n_win, reach


def _attn_c(qT, k, vT, kn, slopes2, lam_vecs, g_head, lambda_init, heads, window):
    B, _, S = qT.shape
    tq, tk = min(Q_TILE, S), min(KV_CHUNK, S)
    tstep = tq * math.gcd(S // tq, Q_SUBTILES)
    smem = pl.BlockSpec(memory_space=pltpu.SMEM)
    h0 = heads[0]
    return pl.pallas_call(
        functools.partial(_diff_kernel, nchunks=S // tk, tk=tk, tq=tq, head0=h0, window=window,
                          out_scale=1.0 - lambda_init, lambda_init=lambda_init),
        grid=(B, len(heads), S // tstep),
        in_specs=[smem,
                  pl.BlockSpec((4, 32), lambda b, h, i: (0, 0)),
                  pl.BlockSpec((1, 64, tstep), lambda b, h, i: (b, h0 + h, i)),
                  pl.BlockSpec((1, 1, S, 128), lambda b, h, i: (b, (h0 + h) // 2, 0, 0)),
                  pl.BlockSpec((1, 64, S), lambda b, h, i: (b, h0 + h, 0)),
                  pl.BlockSpec((1, 1, 2, S), lambda b, h, i: (b, h0 + h, 0, 0)),
                  pl.BlockSpec(g_head.shape, lambda b, h, i: (0, 0))],
        out_specs=pl.BlockSpec((1, 64, tstep), lambda b, h, i: (b, h, i)),
        out_shape=jax.ShapeDtypeStruct((B, 64 * len(heads), S), F32),
        scratch_shapes=[pltpu.VMEM((2 + tq // tk, tk, tq), F32)],
        compiler_params=pltpu.CompilerParams(vmem_limit_bytes=VMEM_LIMIT),
        name="attn_diff",
    )(slopes2, lam_vecs, qT, k, vT, kn, g_head)


def _na_kernel(bmax_ref, q_ref, k_ref, v_ref, kn_ref, bias_ref, o_ref, *, n_rows):
    h = pl.program_id(0)
    n_blocks = n_rows // 4
    nk = NA_SLAB_ROWS * GRID_W
    tq = NA_Q_TILE
    dv = v_ref.shape[1]
    ones = jnp.ones((16, nk), BF16)
    kmax2 = jnp.max(kn_ref[0, 0], axis=1, keepdims=True)
    bmax = bmax_ref[h]

    def variant(a):
        return 0 if a == 0 else (2 if a == n_blocks - 1 else 1)

    def slab(a):
        return min(max(4 * a - 4, 0), n_rows - NA_SLAB_ROWS) * GRID_W

    def q_block(a):
        return q_ref[0, :, a * tq:(a + 1) * tq].astype(F32)

    def scores(a):
        qp = _pad_rows(q_block(a), h % 2, 2).astype(BF16)
        return _dot(k_ref[0, 0, slab(a):slab(a) + nk, :], qp) + bias_ref[0, variant(a)]

    def pv(a, p):
        vaug = jnp.concatenate([v_ref[0, :, slab(a):slab(a) + nk], ones], axis=0)
        return _dot(vaug, p)

    lmin = None
    nxt = scores(0)
    for a in range(n_blocks):
        s = nxt
        if a + 1 < n_blocks:
            nxt = scores(a + 1)
        q32 = q_block(a)
        shift = jnp.sqrt(jnp.sum(q32 * q32, axis=0, keepdims=True) * kmax2) + bmax
        acc = pv(a, jnp.exp2(s - shift).astype(BF16))
        o_ref[0, :, a * tq:(a + 1) * tq] = acc[0:dv] / acc[dv:dv + 1]
        amin = jnp.min(acc[dv:dv + 1])
        lmin = amin if lmin is None else jnp.minimum(lmin, amin)

    @pl.when(jnp.logical_not(lmin >= SHIFT_OK_MIN_SUM))
    def _():
        for a in range(n_blocks):
            s = scores(a)
            p = jnp.exp2(s - jnp.max(s, axis=0, keepdims=True)).astype(BF16)
            acc = pv(a, p)
            o_ref[0, :, a * tq:(a + 1) * tq] = acc[0:dv] / acc[dv:dv + 1]


def _na_bias_tables(rel_bias, n_rows):
    n_blocks = n_rows // 4
    kr_w = min(NA_ROWS, n_rows)
    col = np.arange(GRID_W)
    c0 = np.clip(col - NA_COLS // 2, 0, GRID_W - NA_COLS)
    col_ok = (col[:, None] >= c0[None, :]) & (col[:, None] < c0[None, :] + NA_COLS)
    dc = col[:, None] - col[None, :] + (NA_COLS - 1)
    onehot_c = ((dc[None] == np.arange(2 * NA_COLS - 1)[:, None, None]) & col_ok[None]).astype(np.float32)

    def row_geometry(a):
        R0 = int(np.clip(4 * a - 4, 0, n_rows - NA_SLAB_ROWS))
        r = 4 * a + np.arange(4)
        r0 = np.clip(r - kr_w // 2, 0, n_rows - kr_w)
        kr = R0 + np.arange(NA_SLAB_ROWS)
        ok = (kr[:, None] >= r0[None, :]) & (kr[:, None] < r0[None, :] + kr_w)
        dr = kr[:, None] - r[None, :] + (NA_ROWS - 1)
        return ((dr[:, :, None] == np.arange(2 * NA_ROWS - 1)) & ok[:, :, None]), ok

    variants = [row_geometry(0), row_geometry(1), row_geometry(n_blocks - 1)]
    for a in range(1, n_blocks - 1):
        assert np.array_equal(row_geometry(a)[0], variants[1][0])
    onehot_r = np.stack([v[0] for v in variants]).astype(np.float32)
    valid = (np.stack([v[1] for v in variants])[:, :, None, :, None]
             & col_ok[None, None, :, None, :])
    assert (valid.reshape(3, -1, NA_Q_TILE).sum(axis=1) == kr_w * NA_COLS).all()
    hi = lax.Precision.HIGHEST
    t1 = jnp.einsum('hrd,dkc->hrkc', rel_bias, onehot_c, precision=hi)
    tab = jnp.einsum('viqr,hrkc->hvikqc', onehot_r, t1, precision=hi) * LOG2E
    tab = jnp.where(valid[None], tab, NEG_BIG)
    return tab.reshape(4, 3, NA_SLAB_ROWS * GRID_W, NA_Q_TILE).astype(F32)


def _attn_d(qT, k, vT, kn, bias_tab, bias_max):
    B, _, S = qT.shape
    n_rows = S // GRID_W
    nk = NA_SLAB_ROWS * GRID_W
    return pl.pallas_call(
        functools.partial(_na_kernel, n_rows=n_rows),
        grid=(4, B),
        in_specs=[pl.BlockSpec(memory_space=pltpu.SMEM),
                  pl.BlockSpec((1, 64, S), lambda h, b: (b, h, 0)),
                  pl.BlockSpec((1, 1, S, 128), lambda h, b: (b, h // 2, 0, 0)),
                  pl.BlockSpec((1, 64, S), lambda h, b: (b, h, 0)),
                  pl.BlockSpec((1, 1, 1, S), lambda h, b: (b, h, 0, 0)),
                  pl.BlockSpec((1, 3, nk, NA_Q_TILE), lambda h, b: (h, 0, 0, 0))],
        out_specs=pl.BlockSpec((1, 64, S), lambda h, b: (b, h, 0)),
        out_shape=jax.ShapeDtypeStruct((B, 256, S), F32),
        compiler_params=pltpu.CompilerParams(vmem_limit_bytes=VMEM_LIMIT),
        name="attn_nbr",
    )(bias_max, qT, k, vT, kn, bias_tab)


def _post_kernel(x_ref, oa_ref, ob_ref, od_ref, *rest, ff_chunk, n_parts, n_oc, token_major_out):
    oc_refs = rest[:n_oc]
    (ga_ref, gb_ref, gd_ref, wout_ref, gpost_ref, gmlp_ref, wup_ref, wdown_ref, gmlp_post_ref,
     y_ref) = rest[n_oc:]
    tm = x_ref.shape[2]
    parts = [slice(s * tm // n_parts, (s + 1) * tm // n_parts) for s in range(n_parts)]
    n_ff = wup_ref.shape[0] // ff_chunk

    def mixer_out(cols):
        cat = jnp.concatenate([
            _rms_rows(oa_ref[0, :, cols], ga_ref[...]).astype(BF16),
            _rms_rows(ob_ref[0, :, cols], gb_ref[...]).astype(BF16),
            *[oc_ref[0, :, cols].astype(BF16) for oc_ref in oc_refs],
            _rms_rows(od_ref[0, :, cols], gd_ref[...]).astype(BF16)], axis=0)
        return _dot(wout_ref[...], cat)

    def mlp(h):
        f = None
        for j in range(n_ff):
            up = _dot(wup_ref[j * ff_chunk:(j + 1) * ff_chunk, :], h)
            act = jnp.square(jnp.maximum(up, 0.0)).astype(BF16)
            down = _dot(wdown_ref[:, j * ff_chunk:(j + 1) * ff_chunk], act)
            f = down if f is None else f + down
        return f

    mix = [mixer_out(cols) for cols in parts]
    x1, f = [], []
    for s, cols in enumerate(parts):
        x1.append(x_ref[0, :, cols] + _rms_rows(mix[s], gpost_ref[...]))
        f.append(mlp(_rms_rows(x1[s], gmlp_ref[...]).astype(BF16)))
    for s, cols in enumerate(parts):
        y = x1[s] + _rms_rows(f[s], gmlp_post_ref[...])
        if token_major_out:
            y_ref[0, cols, :] = y.T
        else:
            y_ref[0, :, cols] = y


def _post(xT, oa, ob, oc_parts, od, ga, gb, gd, woutT, gpost, gmlp, wupT, wdownT, gmlp_post,
          token_major_out):
    B, D, S = xT.shape
    tm = min(MLP_TOKEN_TILE, S)
    rows_spec = lambda rows: pl.BlockSpec((1, rows, tm), lambda b, i: (b, 0, i))
    consts = (ga, gb, gd, woutT, gpost, gmlp, wupT, wdownT, gmlp_post)
    out_spec, out_dims = rows_spec(D), (B, D, S)
    if token_major_out:
        out_spec, out_dims = pl.BlockSpec((1, tm, D), lambda b, i: (b, i, 0)), (B, S, D)
    return pl.pallas_call(
        functools.partial(_post_kernel, ff_chunk=1024, n_parts=tm // math.gcd(tm, MLP_PART),
                          n_oc=len(oc_parts), token_major_out=token_major_out),
        grid=(B, S // tm),
        in_specs=([rows_spec(D)] + [rows_spec(256)] * 3 + [rows_spec(o.shape[1]) for o in oc_parts]
                  + [_const_spec(c.shape) for c in consts]),
        out_specs=out_spec,
        out_shape=jax.ShapeDtypeStruct(out_dims, F32),
        compiler_params=pltpu.CompilerParams(vmem_limit_bytes=VMEM_LIMIT),
        name="outproj_mlp",
    )(xT, oa, ob, od, *oc_parts, *consts)


def _rope_tables(n_tokens):
    def angles(pos, dim):
        inv = ROPE_BASE ** (-jnp.arange(0, dim, 2, dtype=F32) / dim)
        return pos.astype(F32)[:, None] * inv[None, :]

    t = jnp.arange(n_tokens, dtype=jnp.int32)
    out = []
    for ang in (angles(t // GRID_W, HEAD_DIM // 2), angles(t % GRID_W, HEAD_DIM // 2),
                angles(t, MLA_ROPE)):
        out += [jnp.cos(ang).T, jnp.sin(ang).T]
    return out


def kernel(x, norm_mix_pre, norm_mix_post, norm_mlp_pre, norm_mlp_post, w_in, a_q_norm, a_k_norm,
           b_cq_norm, b_ckv_norm, b_w_uq, b_w_ukv, c_lambda_q1, c_lambda_k1, c_lambda_q2,
           c_lambda_k2, d_rel_bias, g_out_a, g_out_b, g_out_c, g_out_d, w_out, w_up, w_down):
    B, S, D = x.shape
    depth = w_in.shape[0]
    col = lambda v: jnp.broadcast_to(v.astype(F32).reshape(-1, 1), (v.size, LANES))
    wt = lambda w: w.T.astype(BF16)
    tables = _rope_tables(S)
    slopes = [2.0 ** (-8.0 * (hd + 1) / 4) for hd in range(4)]
    slopes2 = jnp.asarray(slopes, F32) * LOG2E
    tq, tk = min(Q_TILE, S), min(KV_CHUNK, S)
    windows = [_alibi_window(sl * LOG2E, S, tq, tk) for sl in slopes]
    head_groups = []
    for hd in range(4):
        if head_groups and windows[hd] is None and head_groups[-1][1] is None:
            head_groups[-1][0].append(hd)
        else:
            head_groups.append(([hd], windows[hd]))
    xT = x
    for l in range(depth):
        (qa, ka, va, qb, kb, vb, qc, kc, vc, qd, kd, vd, kna, knb, knc, knd, *x_feature_major) = _inproj(
            xT, col(norm_mix_pre[l]), wt(w_in[l]), col(a_q_norm[l]), col(a_k_norm[l]),
            col(b_cq_norm[l]), col(b_ckv_norm[l]), wt(b_w_uq[l]), wt(b_w_ukv[l]), tables,
            token_major=(l == 0))
        if l == 0:
            (xT,) = x_feature_major
        lambda_init = 0.8 - 0.6 * math.exp(-0.3 * l)
        lam_vecs = jnp.stack([c_lambda_q1[l], c_lambda_k1[l], c_lambda_q2[l], c_lambda_k2[l]]).astype(F32)
        oa = _attn_a(qa, ka, va, kna)
        ob = _attn_b(qb, kb, vb, knb)
        oc = [_attn_c(qc, kc, vc, knc, slopes2, lam_vecs, col(g_out_c[l]), lambda_init, tuple(hs), win)
              for hs, win in head_groups]
        rel_bias = d_rel_bias[l].astype(F32)
        od = _attn_d(qd, kd, vd, knd, _na_bias_tables(rel_bias, S // GRID_W),
                     jnp.max(rel_bias, axis=(1, 2)) * LOG2E)
        xT = _post(xT, oa, ob, oc, od, col(g_out_a[l]), col(g_out_b[l]), col(g_out_d[l]),
                   wt(w_out[l]), col(norm_mix_post[l]), col(norm_mlp_pre[l]), wt(w_up[l]),
                   wt(w_down[l]), col(norm_mlp_post[l]), token_major_out=(l == depth - 1))
    return xT
```

```python
import functools
import math

import numpy as np
import jax
import jax.numpy as jnp
from jax import lax
from jax.experimental import pallas as pl
from jax.experimental.pallas import tpu as pltpu

GRID_W = 64
HEAD_DIM = 64
GROUP_W = 256
N_KV_A = 2
MLA_Q_RANK = 192
MLA_KV_RANK = 128
MLA_NOPE = 64
MLA_ROPE = 32
DIFF_QK = 32
NA_ROWS = 8
NA_COLS = 16
ROPE_BASE = 10000.0
EPS = 1e-6
LOG2E = math.log2(math.e)
NEG_BIG = -1e30

A_COLS = 512
B_COLS = MLA_Q_RANK + MLA_KV_RANK + MLA_ROPE
C_COLS = 768
D_COLS = 768
OFF_B = A_COLS
OFF_C = OFF_B + B_COLS
OFF_D = OFF_C + C_COLS
IN_COLS = OFF_D + D_COLS

TOKEN_TILE = 512
MLP_TOKEN_TILE = 512
MLP_PART = 512
Q_TILE = 512
Q_SUBTILES = 4
KV_CHUNK = 256
QK_LOOKAHEAD = 2
KV_UNROLL = 16
NA_Q_TILE = 256
NA_SLAB_ROWS = 12
VMEM_LIMIT = 56 * 1024 * 1024

F32 = jnp.float32
BF16 = jnp.bfloat16
LANES = 128


def _dot(a, b):
    return jnp.dot(a, b, preferred_element_type=F32)


def _rms_rows(x, gain):
    gain = jnp.concatenate([gain] * (x.shape[1] // LANES), axis=1)
    ms = jnp.mean(x * x, axis=0, keepdims=True)
    return x * lax.rsqrt(ms + EPS) * gain


def _rope_rows(x, cos, sin):
    n = cos.shape[0]
    x1, x2 = x[:n], x[n:]
    return x1 * cos - x2 * sin, x1 * sin + x2 * cos


def _inproj_kernel(x_ref, gpre_ref, w_ref, aq_ref, ak_ref, cqg_ref, ckvg_ref, wuq_ref, wukv_ref,
                   cr_ref, sr_ref, cc_ref, sc_ref, cs_ref, ss_ref,
                   qa_ref, ka_ref, va_ref, qb_ref, kb_ref, vb_ref,
                   qc_ref, kc_ref, vc_ref, qd_ref, kd_ref, vd_ref,
                   kna_ref, knb_ref, knc_ref, knd_ref, *xt_ref):
    if xt_ref:
        x = x_ref[0].T
        xt_ref[0][0] = x
    else:
        x = x_ref[0]
    h = _rms_rows(x, gpre_ref[...]).astype(BF16)
    cr, sr, cc, sc = cr_ref[...], sr_ref[...], cc_ref[...], sc_ref[...]
    cs, ss = cs_ref[...], ss_ref[...]
    tm = x.shape[1]

    def axial(t):
        r1, r2 = _rope_rows(t[0:32], cr, sr)
        c1, c2 = _rope_rows(t[32:64], cc, sc)
        return r1, r2, c1, c2

    def sumsq(t):
        return jnp.sum(t * t, axis=0, keepdims=True)

    ua = _dot(w_ref[0:A_COLS, :], h)
    qscale = HEAD_DIM ** -0.5 * LOG2E
    for hd in range(4):
        t = _rms_rows(ua[64 * hd:64 * hd + 64], aq_ref[...])
        for j, piece in enumerate(axial(t)):
            qa_ref[0, 64 * hd + 16 * j:64 * hd + 16 * j + 16, :] = (piece * qscale).astype(BF16)
    kparts = []
    for g in range(N_KV_A):
        t = _rms_rows(ua[256 + 64 * g:256 + 64 * g + 64], ak_ref[...])
        kparts.extend(axial(t))
        kna_ref[0, g] = sumsq(t)
    ka_ref[0] = jnp.concatenate(kparts, axis=0).T.astype(BF16)
    va_ref[0] = ua[384:512].astype(BF16)

    ub = _dot(w_ref[OFF_B:OFF_B + B_COLS, :], h)
    cqn = _rms_rows(ub[0:MLA_Q_RANK], cqg_ref[...]).astype(BF16)
    ckvn = _rms_rows(ub[MLA_Q_RANK:MLA_Q_RANK + MLA_KV_RANK], ckvg_ref[...]).astype(BF16)
    kr1, kr2 = _rope_rows(ub[MLA_Q_RANK + MLA_KV_RANK:B_COLS], cs, ss)
    qb = _dot(wuq_ref[...], cqn)
    kvb = _dot(wukv_ref[...], ckvn)
    bscale = (MLA_NOPE + MLA_ROPE) ** -0.5 * LOG2E
    zeros32 = jnp.zeros((32, tm), F32)
    for hd in range(4):
        base = 96 * hd
        q1, q2 = _rope_rows(qb[base + 64:base + 96], cs, ss)
        qb_ref[0, 128 * hd:128 * hd + 64, :] = (qb[base:base + 64] * bscale).astype(BF16)
        qb_ref[0, 128 * hd + 64:128 * hd + 80, :] = (q1 * bscale).astype(BF16)
        qb_ref[0, 128 * hd + 80:128 * hd + 96, :] = (q2 * bscale).astype(BF16)
        qb_ref[0, 128 * hd + 96:128 * hd + 128, :] = jnp.zeros((32, tm), BF16)
        kfull = jnp.concatenate([kvb[128 * hd:128 * hd + 64], kr1, kr2, zeros32], axis=0)
        kb_ref[0, hd] = kfull.T.astype(BF16)
        knb_ref[0, hd] = sumsq(kfull)
        vb_ref[0, 64 * hd:64 * hd + 64, :] = kvb[128 * hd + 64:128 * hd + 128].astype(BF16)

    uc = _dot(w_ref[OFF_C:OFF_C + C_COLS, :], h)
    qc_ref[0] = (uc[0:256] * (DIFF_QK ** -0.5 * LOG2E)).astype(BF16)
    for p in range(2):
        kc_ref[0, p] = uc[256 + 128 * p:256 + 128 * p + 128].T.astype(BF16)
    vc_ref[0] = uc[512:768].astype(BF16)
    for hd in range(4):
        for mp in range(2):
            base = 256 + 64 * hd + 32 * mp
            knc_ref[0, hd, mp:mp + 1, :] = sumsq(uc[base:base + 32])

    ud = _dot(w_ref[OFF_D:OFF_D + D_COLS, :], h)
    qd_ref[0] = (ud[0:256] * (HEAD_DIM ** -0.5 * LOG2E)).astype(BF16)
    for p in range(2):
        kd_ref[0, p] = ud[256 + 128 * p:256 + 128 * p + 128].T.astype(BF16)
    vd_ref[0] = ud[512:768].astype(BF16)
    for hd in range(4):
        knd_ref[0, hd] = sumsq(ud[256 + 64 * hd:256 + 64 * hd + 64])


def _const_spec(shape):
    nd = len(shape)
    return pl.BlockSpec(shape, lambda *_: (0,) * nd, pipeline_mode=pl.Buffered(1))


def _inproj(x, gpre, wT, aq, ak, cqg, ckvg, wuqT, wukvT, tables, token_major):
    B, D, S = (x.shape[0], x.shape[2], x.shape[1]) if token_major else x.shape
    tm = min(TOKEN_TILE, S)
    grid = (B, S // tm)
    tab_spec = pl.BlockSpec((16, tm), lambda b, i: (0, i))
    rows_spec = lambda rows: pl.BlockSpec((1, rows, tm), lambda b, i: (b, 0, i))
    out_shape = (
        jax.ShapeDtypeStruct((B, 256, S), BF16),
        jax.ShapeDtypeStruct((B, S, 128), BF16),
        jax.ShapeDtypeStruct((B, 128, S), BF16),
        jax.ShapeDtypeStruct((B, 512, S), BF16),
        jax.ShapeDtypeStruct((B, 4, S, 128), BF16),
        jax.ShapeDtypeStruct((B, 256, S), BF16),
        jax.ShapeDtypeStruct((B, 256, S), BF16),
        jax.ShapeDtypeStruct((B, 2, S, 128), BF16),
        jax.ShapeDtypeStruct((B, 256, S), BF16),
        jax.ShapeDtypeStruct((B, 256, S), BF16),
        jax.ShapeDtypeStruct((B, 2, S, 128), BF16),
        jax.ShapeDtypeStruct((B, 256, S), BF16),
        jax.ShapeDtypeStruct((B, 2, 1, S), F32),
        jax.ShapeDtypeStruct((B, 4, 1, S), F32),
        jax.ShapeDtypeStruct((B, 4, 2, S), F32),
        jax.ShapeDtypeStruct((B, 4, 1, S), F32),
    )
    kspec = lambda n: pl.BlockSpec((1, n, tm, 128), lambda b, i: (b, 0, i, 0))
    nspec = lambda n, r: pl.BlockSpec((1, n, r, tm), lambda b, i: (b, 0, 0, i))
    out_specs = (
        rows_spec(256), pl.BlockSpec((1, tm, 128), lambda b, i: (b, i, 0)), rows_spec(128),
        rows_spec(512), kspec(4), rows_spec(256),
        rows_spec(256), kspec(2), rows_spec(256),
        rows_spec(256), kspec(2), rows_spec(256),
        nspec(2, 1), nspec(4, 1), nspec(4, 2), nspec(4, 1),
    )
    x_spec = rows_spec(D)
    if token_major:
        x_spec = pl.BlockSpec((1, tm, D), lambda b, i: (b, i, 0))
        out_shape += (jax.ShapeDtypeStruct((B, D, S), F32),)
        out_specs += (rows_spec(D),)
    in_specs = [
        x_spec, _const_spec(gpre.shape), _const_spec(wT.shape),
        _const_spec(aq.shape), _const_spec(ak.shape), _const_spec(cqg.shape), _const_spec(ckvg.shape),
        _const_spec(wuqT.shape), _const_spec(wukvT.shape),
    ] + [tab_spec] * 6
    return pl.pallas_call(
        _inproj_kernel, grid=grid, in_specs=in_specs, out_specs=out_specs, out_shape=out_shape,
        compiler_params=pltpu.CompilerParams(vmem_limit_bytes=VMEM_LIMIT),
        name="inproj",
    )(x, gpre, wT, aq, ak, cqg, ckvg, wuqT, wukvT, *tables)


def _pad_rows(q32, slot, nslots):
    return jnp.concatenate([jnp.where(slot == j, q32, 0.0) for j in range(nslots)], axis=0)


def _softmax_stage(t, cst, m):
    m_new = jnp.maximum(m, jnp.max(t, axis=0, keepdims=True) + cst)
    alpha = jnp.exp2(m - m_new)
    p = jnp.exp2(t - (m_new - cst))
    return m_new, alpha, p


def _fold_rows(p):
    slabs = [p[r:r + 8] for r in range(0, p.shape[0], 8)]
    chains = [functools.reduce(jnp.add, slabs[w::4]) for w in range(min(4, len(slabs)))]
    return functools.reduce(jnp.add, chains)


SHIFT_OK_MIN_SUM = 2.0 ** -64
ZERO_PROB_LOG2 = 140.0


def _attention(nchunks, tq, n_sub, n_maps, shift_of, qk_stage, pv_stage, finalize, window=None):
    dv = pv_stage.rows
    zero_acc = lambda: tuple(jnp.zeros((dv, tq), F32) for _ in range(n_maps))
    zero_sum = lambda: tuple(jnp.zeros((8, tq), F32) for _ in range(n_maps))

    def run_tasks(tasks, accs, sums):
        queue = []
        for idx, (sub, c, j) in enumerate(tasks):
            while len(queue) + idx < min(idx + QK_LOOKAHEAD + 1, len(tasks)):
                ahead_sub, ahead_c, ahead_j = tasks[idx + len(queue)]
                queue.append(qk_stage(ahead_c, ahead_j, ahead_sub))
            tile, cst = queue.pop(0)
            p = jnp.exp2(tile - (shift_of(sub, j) - cst))
            sums[sub][j] = sums[sub][j] + _fold_rows(p)
            accs[sub][j] = accs[sub][j] + pv_stage(c, p.astype(BF16))

    unroll = math.gcd(nchunks, KV_UNROLL)
    accs = [list(zero_acc()) for _ in range(n_sub)]
    sums = [list(zero_sum()) for _ in range(n_sub)]
    if window is not None:
        n_win, first = window
        starts = [first(sub) for sub in range(n_sub)]
        run_tasks([(sub, starts[sub] + w, j) for sub in range(n_sub) for w in range(n_win)
                   for j in range(n_maps)], accs, sums)
    elif unroll == nchunks:
        run_tasks([(sub, c, j) for sub in range(n_sub) for c in range(nchunks)
                   for j in range(n_maps)], accs, sums)
    else:
        for sub in range(n_sub):
            for j in range(n_maps):
                shift_of(sub, j)

            def trip(t, carry, sub=sub):
                acc, ssum = {sub: list(carry[0])}, {sub: list(carry[1])}
                run_tasks([(sub, t * unroll + u, j) for u in range(unroll) for j in range(n_maps)],
                          acc, ssum)
                return tuple(acc[sub]), tuple(ssum[sub])
            done = lax.fori_loop(0, nchunks // unroll, trip, (tuple(accs[sub]), tuple(sums[sub])))
            accs[sub], sums[sub] = list(done[0]), list(done[1])

    lmins = []
    for sub in range(n_sub):
        denom = [jnp.sum(s8, axis=0, keepdims=True) for s8 in sums[sub]]
        finalize(sub, list(zip(accs[sub], denom)))
        lmins += [jnp.min(d) for d in denom]

    @pl.when(jnp.logical_not(functools.reduce(jnp.minimum, lmins) >= SHIFT_OK_MIN_SUM))
    def _():
        for sub in range(n_sub):
            def online_trip(c, carry, sub=sub):
                out = []
                for j, (m, l, acc) in enumerate(carry):
                    tile, cst = qk_stage(c, j, sub)
                    m_new, alpha, p = _softmax_stage(tile, cst, m)
                    out.append((m_new, alpha * l + jnp.sum(p, axis=0, keepdims=True),
                                alpha * acc + pv_stage(c, p.astype(BF16))))
                return tuple(out)

            init = tuple((jnp.full((1, tq), NEG_BIG, F32), jnp.zeros((1, tq), F32), acc0)
                         for acc0 in zero_acc())
            done = lax.fori_loop(0, nchunks, online_trip, init)
            finalize(sub, [(acc, l) for _, l, acc in done])


def _chunk_slice(c, tk):
    return pl.ds(c * tk if isinstance(c, int) else pl.multiple_of(c * tk, tk), tk)


def _make_pv_stage(v_ref, tk):
    def pv_stage(c, p):
        return _dot(v_ref[0, :, _chunk_slice(c, tk)], p)

    pv_stage.rows = v_ref.shape[1]
    return pv_stage


BOUND_SLACK = 1.01


def _score_bound(q32, kn2):
    return BOUND_SLACK * jnp.sqrt(jnp.sum(q32 * q32, axis=0, keepdims=True)
                                  * jnp.max(kn2, axis=1, keepdims=True))


def _attn_kernel(q_ref, k_ref, v_ref, kn_ref, o_ref, *, k_slots, nchunks, tk, tq):
    n_sub = q_ref.shape[2] // tq

    @functools.cache
    def queries(sub):
        q32 = q_ref[0, :, sub * tq:(sub + 1) * tq].astype(F32)
        shift = _score_bound(q32, kn_ref[0, 0])
        if k_slots > 1:
            q32 = _pad_rows(q32, pl.program_id(1) // 2, k_slots)
        return q32.astype(BF16), shift

    k_at = (lambda sl: k_ref[0, sl, :]) if len(k_ref.shape) == 3 else (lambda sl: k_ref[0, 0, sl, :])

    def qk_stage(c, j, sub):
        return _dot(k_at(_chunk_slice(c, tk)), queries(sub)[0]), jnp.float32(0.0)

    def finalize(sub, acc):
        (o, l), = acc
        o_ref[0, :, sub * tq:(sub + 1) * tq] = o / l

    _attention(nchunks, tq, n_sub, 1, lambda sub, j: queries(sub)[1], qk_stage,
               _make_pv_stage(v_ref, tk), finalize)


def _attn_a(qT, k, vT, kn):
    B, _, S = qT.shape
    tq, tk = min(Q_TILE, S), min(KV_CHUNK, S)
    tstep = tq * math.gcd(S // tq, Q_SUBTILES)
    return pl.pallas_call(
        functools.partial(_attn_kernel, k_slots=2, nchunks=S // tk, tk=tk, tq=tq),
        grid=(B, 4, S // tstep),
        in_specs=[pl.BlockSpec((1, 64, tstep), lambda b, h, i: (b, h, i)),
                  pl.BlockSpec((1, S, 128), lambda b, h, i: (b, 0, 0)),
                  pl.BlockSpec((1, 64, S), lambda b, h, i: (b, h // 2, 0)),
                  pl.BlockSpec((1, 1, 1, S), lambda b, h, i: (b, h // 2, 0, 0))],
        out_specs=pl.BlockSpec((1, 64, tstep), lambda b, h, i: (b, h, i)),
        out_shape=jax.ShapeDtypeStruct((B, 256, S), F32),
        compiler_params=pltpu.CompilerParams(vmem_limit_bytes=VMEM_LIMIT),
        name="attn_gqa",
    )(qT, k, vT, kn)


def _attn_b(qT, k, vT, kn):
    B, _, S = qT.shape
    tq, tk = min(Q_TILE, S), min(KV_CHUNK, S)
    tstep = tq * math.gcd(S // tq, Q_SUBTILES)
    return pl.pallas_call(
        functools.partial(_attn_kernel, k_slots=1, nchunks=S // tk, tk=tk, tq=tq),
        grid=(B, 4, S // tstep),
        in_specs=[pl.BlockSpec((1, 128, tstep), lambda b, h, i: (b, h, i)),
                  pl.BlockSpec((1, 1, S, 128), lambda b, h, i: (b, h, 0, 0)),
                  pl.BlockSpec((1, 64, S), lambda b, h, i: (b, h, 0)),
                  pl.BlockSpec((1, 1, 1, S), lambda b, h, i: (b, h, 0, 0))],
        out_specs=pl.BlockSpec((1, 64, tstep), lambda b, h, i: (b, h, i)),
        out_shape=jax.ShapeDtypeStruct((B, 256, S), F32),
        compiler_params=pltpu.CompilerParams(vmem_limit_bytes=VMEM_LIMIT),
        name="attn_mla",
    )(qT, k, vT, kn)


def _diff_kernel(slope_ref, lam_ref, q_ref, k_ref, v_ref, kn_ref, g_ref, o_ref, bias_ref,
                 *, nchunks, tk, tq, head0, window, out_scale, lambda_init):
    h = head0 + pl.program_id(1)
    i = pl.program_id(2)
    n_sub = q_ref.shape[2] // tq
    dv = v_ref.shape[1]
    n_inside = tq // tk
    slope = slope_ref[h]

    @pl.when(i == 0)
    def _():
        d = (lax.broadcasted_iota(jnp.int32, (tk, tq), 1)
             - lax.broadcasted_iota(jnp.int32, (tk, tq), 0)).astype(F32)
        bias_ref[0] = -d * slope
        bias_ref[1] = d * slope
        for j in range(n_inside):
            bias_ref[2 + j] = -jnp.abs(d - float(j * tk)) * slope

    slot0 = (h % 2) * 2

    @functools.cache
    def queries(sub, mp):
        q32 = q_ref[0, 32 * mp:32 * mp + 32, sub * tq:(sub + 1) * tq].astype(F32)
        shift = _score_bound(q32, kn_ref[0, 0, mp:mp + 1, :])
        return _pad_rows(q32, slot0 + mp, 4).astype(BF16), shift

    def qk_stage(c, mp, sub):
        kc = k_ref[0, 0, _chunk_slice(c, tk), :]
        gap = (i * n_sub + sub) * tq - c * tk
        inside = jnp.logical_and(gap <= 0, gap > -tq)
        bias = bias_ref[jnp.where(gap > 0, 0, jnp.where(inside, 2 + (-gap) // tk, 1))]
        cst = jnp.where(inside, 0.0, -jnp.abs(gap).astype(F32) * slope)
        return _dot(kc, queries(sub, mp)[0]) + bias, cst

    def finalize(sub, acc):
        (a1, l1), (a2, l2) = acc
        lv = lam_ref[...]
        lam = (jnp.exp(jnp.sum(lv[0:1] * lv[1:2], axis=1, keepdims=True))
               - jnp.exp(jnp.sum(lv[2:3] * lv[3:4], axis=1, keepdims=True)) + lambda_init)
        o = a1 / l1 - lam * (a2 / l2)
        o_ref[0, :, sub * tq:(sub + 1) * tq] = _rms_rows(o, g_ref[...]) * out_scale

    chunk_window = None
    if window is not None:
        n_win, reach = window
        first = lambda sub: jnp.clip(((i * n_sub + sub) * tq - reach) // tk, 0, nchunks - n_win)
        chunk_window = (n_win, first)
    _attention(nchunks, tq, n_sub, 2, lambda sub, mp: queries(sub, mp)[1], qk_stage,
               _make_pv_stage(v_ref, tk), finalize, chunk_window)


def _alibi_window(slope2, n_tokens, tq, tk):
    reach = math.ceil(ZERO_PROB_LOG2 / slope2)
    n_win = -(-(tq + 2 * reach) // tk) + 1
    nchunks = n_tokens // tk
    if n_win >= nchunks:
        return None
    for q0 in range(0, n_tokens, tq):
        first = min(max((q0 - reach) // tk, 0), nchunks - n_win)
        lo, hi = max(q0 - reach, 0) // tk, min(q0 + tq - 1 + reach, n_tokens - 1) // tk
        assert first <= lo and hi < first + n_win
    return n_win, reach


def _attn_c(qT, k, vT, kn, slopes2, lam_vecs, g_head, lambda_init, heads, window):
    B, _, S = qT.shape
    tq, tk = min(Q_TILE, S), min(KV_CHUNK, S)
    tstep = tq * math.gcd(S // tq, Q_SUBTILES)
    smem = pl.BlockSpec(memory_space=pltpu.SMEM)
    h0 = heads[0]
    return pl.pallas_call(
        functools.partial(_diff_kernel, nchunks=S // tk, tk=tk, tq=tq, head0=h0, window=window,
                          out_scale=1.0 - lambda_init, lambda_init=lambda_init),
        grid=(B, len(heads), S // tstep),
        in_specs=[smem,
                  pl.BlockSpec((4, 32), lambda b, h, i: (0, 0)),
                  pl.BlockSpec((1, 64, tstep), lambda b, h, i: (b, h0 + h, i)),
                  pl.BlockSpec((1, 1, S, 128), lambda b, h, i: (b, (h0 + h) // 2, 0, 0)),
                  pl.BlockSpec((1, 64, S), lambda b, h, i: (b, h0 + h, 0)),
                  pl.BlockSpec((1, 1, 2, S), lambda b, h, i: (b, h0 + h, 0, 0)),
                  pl.BlockSpec(g_head.shape, lambda b, h, i: (0, 0))],
        out_specs=pl.BlockSpec((1, 64, tstep), lambda b, h, i: (b, h, i)),
        out_shape=jax.ShapeDtypeStruct((B, 64 * len(heads), S), F32),
        scratch_shapes=[pltpu.VMEM((2 + tq // tk, tk, tq), F32)],
        compiler_params=pltpu.CompilerParams(vmem_limit_bytes=VMEM_LIMIT),
        name="attn_diff",
    )(slopes2, lam_vecs, qT, k, vT, kn, g_head)


def _na_kernel(bmax_ref, q_ref, k_ref, v_ref, kn_ref, bias_ref, o_ref, *, n_rows):
    h = pl.program_id(0)
    n_blocks = n_rows // 4
    nk = NA_SLAB_ROWS * GRID_W
    tq = NA_Q_TILE
    dv = v_ref.shape[1]
    ones = jnp.ones((16, nk), BF16)
    kmax2 = jnp.max(kn_ref[0, 0], axis=1, keepdims=True)
    bmax = bmax_ref[h]

    def variant(a):
        return 0 if a == 0 else (2 if a == n_blocks - 1 else 1)

    def slab(a):
        return min(max(4 * a - 4, 0), n_rows - NA_SLAB_ROWS) * GRID_W

    def q_block(a):
        return q_ref[0, :, a * tq:(a + 1) * tq].astype(F32)

    def scores(a):
        qp = _pad_rows(q_block(a), h % 2, 2).astype(BF16)
        return _dot(k_ref[0, 0, slab(a):slab(a) + nk, :], qp) + bias_ref[0, variant(a)]

    def pv(a, p):
        vaug = jnp.concatenate([v_ref[0, :, slab(a):slab(a) + nk], ones], axis=0)
        return _dot(vaug, p)

    lmin = None
    nxt = scores(0)
    for a in range(n_blocks):
        s = nxt
        if a + 1 < n_blocks:
            nxt = scores(a + 1)
        q32 = q_block(a)
        shift = jnp.sqrt(jnp.sum(q32 * q32, axis=0, keepdims=True) * kmax2) + bmax
        acc = pv(a, jnp.exp2(s - shift).astype(BF16))
        o_ref[0, :, a * tq:(a + 1) * tq] = acc[0:dv] / acc[dv:dv + 1]
        amin = jnp.min(acc[dv:dv + 1])
        lmin = amin if lmin is None else jnp.minimum(lmin, amin)

    @pl.when(jnp.logical_not(lmin >= SHIFT_OK_MIN_SUM))
    def _():
        for a in range(n_blocks):
            s = scores(a)
            p = jnp.exp2(s - jnp.max(s, axis=0, keepdims=True)).astype(BF16)
            acc = pv(a, p)
            o_ref[0, :, a * tq:(a + 1) * tq] = acc[0:dv] / acc[dv:dv + 1]


def _na_bias_tables(rel_bias, n_rows):
    n_blocks = n_rows // 4
    kr_w = min(NA_ROWS, n_rows)
    col = np.arange(GRID_W)
    c0 = np.clip(col - NA_COLS // 2, 0, GRID_W - NA_COLS)
    col_ok = (col[:, None] >= c0[None, :]) & (col[:, None] < c0[None, :] + NA_COLS)
    dc = col[:, None] - col[None, :] + (NA_COLS - 1)
    onehot_c = ((dc[None] == np.arange(2 * NA_COLS - 1)[:, None, None]) & col_ok[None]).astype(np.float32)

    def row_geometry(a):
        R0 = int(np.clip(4 * a - 4, 0, n_rows - NA_SLAB_ROWS))
        r = 4 * a + np.arange(4)
        r0 = np.clip(r - kr_w // 2, 0, n_rows - kr_w)
        kr = R0 + np.arange(NA_SLAB_ROWS)
        ok = (kr[:, None] >= r0[None, :]) & (kr[:, None] < r0[None, :] + kr_w)
        dr = kr[:, None] - r[None, :] + (NA_ROWS - 1)
        return ((dr[:, :, None] == np.arange(2 * NA_ROWS - 1)) & ok[:, :, None]), ok

    variants = [row_geometry(0), row_geometry(1), row_geometry(n_blocks - 1)]
    for a in range(1, n_blocks - 1):
        assert np.array_equal(row_geometry(a)[0], variants[1][0])
    onehot_r = np.stack([v[0] for v in variants]).astype(np.float32)
    valid = (np.stack([v[1] for v in variants])[:, :, None, :, None]
             & col_ok[None, None, :, None, :])
    assert (valid.reshape(3, -1, NA_Q_TILE).sum(axis=1) == kr_w * NA_COLS).all()
    hi = lax.Precision.HIGHEST
    t1 = jnp.einsum('hrd,dkc->hrkc', rel_bias, onehot_c, precision=hi)
    tab = jnp.einsum('viqr,hrkc->hvikqc', onehot_r, t1, precision=hi) * LOG2E
    tab = jnp.where(valid[None], tab, NEG_BIG)
    return tab.reshape(4, 3, NA_SLAB_ROWS * GRID_W, NA_Q_TILE).astype(F32)


def _attn_d(qT, k, vT, kn, bias_tab, bias_max):
    B, _, S = qT.shape
    n_rows = S // GRID_W
    nk = NA_SLAB_ROWS * GRID_W
    return pl.pallas_call(
        functools.partial(_na_kernel, n_rows=n_rows),
        grid=(4, B),
        in_specs=[pl.BlockSpec(memory_space=pltpu.SMEM),
                  pl.BlockSpec((1, 64, S), lambda h, b: (b, h, 0)),
                  pl.BlockSpec((1, 1, S, 128), lambda h, b: (b, h // 2, 0, 0)),
                  pl.BlockSpec((1, 64, S), lambda h, b: (b, h, 0)),
                  pl.BlockSpec((1, 1, 1, S), lambda h, b: (b, h, 0, 0)),
                  pl.BlockSpec((1, 3, nk, NA_Q_TILE), lambda h, b: (h, 0, 0, 0))],
        out_specs=pl.BlockSpec((1, 64, S), lambda h, b: (b, h, 0)),
        out_shape=jax.ShapeDtypeStruct((B, 256, S), F32),
        compiler_params=pltpu.CompilerParams(vmem_limit_bytes=VMEM_LIMIT),
        name="attn_nbr",
    )(bias_max, qT, k, vT, kn, bias_tab)


def _post_kernel(x_ref, oa_ref, ob_ref, od_ref, *rest, ff_chunk, n_parts, n_oc, token_major_out):
    oc_refs = rest[:n_oc]
    (ga_ref, gb_ref, gd_ref, wout_ref, gpost_ref, gmlp_ref, wup_ref, wdown_ref, gmlp_post_ref,
     y_ref) = rest[n_oc:]
    tm = x_ref.shape[2]
    parts = [slice(s * tm // n_parts, (s + 1) * tm // n_parts) for s in range(n_parts)]
    n_ff = wup_ref.shape[0] // ff_chunk

    def mixer_out(cols):
        cat = jnp.concatenate([
            _rms_rows(oa_ref[0, :, cols], ga_ref[...]).astype(BF16),
            _rms_rows(ob_ref[0, :, cols], gb_ref[...]).astype(BF16),
            *[oc_ref[0, :, cols].astype(BF16) for oc_ref in oc_refs],
            _rms_rows(od_ref[0, :, cols], gd_ref[...]).astype(BF16)], axis=0)
        return _dot(wout_ref[...], cat)

    def mlp(h):
        f = None
        for j in range(n_ff):
            up = _dot(wup_ref[j * ff_chunk:(j + 1) * ff_chunk, :], h)
            act = jnp.square(jnp.maximum(up, 0.0)).astype(BF16)
            down = _dot(wdown_ref[:, j * ff_chunk:(j + 1) * ff_chunk], act)
            f = down if f is None else f + down
        return f

    mix = [mixer_out(cols) for cols in parts]
    x1, f = [], []
    for s, cols in enumerate(parts):
        x1.append(x_ref[0, :, cols] + _rms_rows(mix[s], gpost_ref[...]))
        f.append(mlp(_rms_rows(x1[s], gmlp_ref[...]).astype(BF16)))
    for s, cols in enumerate(parts):
        y = x1[s] + _rms_rows(f[s], gmlp_post_ref[...])
        if token_major_out:
            y_ref[0, cols, :] = y.T
        else:
            y_ref[0, :, cols] = y


def _post(xT, oa, ob, oc_parts, od, ga, gb, gd, woutT, gpost, gmlp, wupT, wdownT, gmlp_post,
          token_major_out):
    B, D, S = xT.shape
    tm = min(MLP_TOKEN_TILE, S)
    rows_spec = lambda rows: pl.BlockSpec((1, rows, tm), lambda b, i: (b, 0, i))
    consts = (ga, gb, gd, woutT, gpost, gmlp, wupT, wdownT, gmlp_post)
    out_spec, out_dims = rows_spec(D), (B, D, S)
    if token_major_out:
        out_spec, out_dims = pl.BlockSpec((1, tm, D), lambda b, i: (b, i, 0)), (B, S, D)
    return pl.pallas_call(
        functools.partial(_post_kernel, ff_chunk=1024, n_parts=tm // math.gcd(tm, MLP_PART),
                          n_oc=len(oc_parts), token_major_out=token_major_out),
        grid=(B, S // tm),
        in_specs=([rows_spec(D)] + [rows_spec(256)] * 3 + [rows_spec(o.shape[1]) for o in oc_parts]
                  + [_const_spec(c.shape) for c in consts]),
        out_specs=out_spec,
        out_shape=jax.ShapeDtypeStruct(out_dims, F32),
        compiler_params=pltpu.CompilerParams(vmem_limit_bytes=VMEM_LIMIT),
        name="outproj_mlp",
    )(xT, oa, ob, od, *oc_parts, *consts)


def _rope_tables(n_tokens):
    def angles(pos, dim):
        inv = ROPE_BASE ** (-jnp.arange(0, dim, 2, dtype=F32) / dim)
        return pos.astype(F32)[:, None] * inv[None, :]

    t = jnp.arange(n_tokens, dtype=jnp.int32)
    out = []
    for ang in (angles(t // GRID_W, HEAD_DIM // 2), angles(t % GRID_W, HEAD_DIM // 2),
                angles(t, MLA_ROPE)):
        out += [jnp.cos(ang).T, jnp.sin(ang).T]
    return out


def kernel(x, norm_mix_pre, norm_mix_post, norm_mlp_pre, norm_mlp_post, w_in, a_q_norm, a_k_norm,
           b_cq_norm, b_ckv_norm, b_w_uq, b_w_ukv, c_lambda_q1, c_lambda_k1, c_lambda_q2,
           c_lambda_k2, d_rel_bias, g_out_a, g_out_b, g_out_c, g_out_d, w_out, w_up, w_down):
    B, S, D = x.shape
    depth = w_in.shape[0]
    col = lambda v: jnp.broadcast_to(v.astype(F32).reshape(-1, 1), (v.size, LANES))
    wt = lambda w: w.T.astype(BF16)
    tables = _rope_tables(S)
    slopes = [2.0 ** (-8.0 * (hd + 1) / 4) for hd in range(4)]
    slopes2 = jnp.asarray(slopes, F32) * LOG2E
    tq, tk = min(Q_TILE, S), min(KV_CHUNK, S)
    windows = [_alibi_window(sl * LOG2E, S, tq, tk) for sl in slopes]
    head_groups = []
    for hd in range(4):
        if head_groups and windows[hd] is None and head_groups[-1][1] is None:
            head_groups[-1][0].append(hd)
        else:
            head_groups.append(([hd], windows[hd]))
    xT = x
    for l in range(depth):
        (qa, ka, va, qb, kb, vb, qc, kc, vc, qd, kd, vd, kna, knb, knc, knd, *x_feature_major) = _inproj(
            xT, col(norm_mix_pre[l]), wt(w_in[l]), col(a_q_norm[l]), col(a_k_norm[l]),
            col(b_cq_norm[l]), col(b_ckv_norm[l]), wt(b_w_uq[l]), wt(b_w_ukv[l]), tables,
            token_major=(l == 0))
        if l == 0:
            (xT,) = x_feature_major
        lambda_init = 0.8 - 0.6 * math.exp(-0.3 * l)
        lam_vecs = jnp.stack([c_lambda_q1[l], c_lambda_k1[l], c_lambda_q2[l], c_lambda_k2[l]]).astype(F32)
        oa = _attn_a(qa, ka, va, kna)
        ob = _attn_b(qb, kb, vb, knb)
        oc = [_attn_c(qc, kc, vc, knc, slopes2, lam_vecs, col(g_out_c[l]), lambda_init, tuple(hs), win)
              for hs, win in head_groups]
        rel_bias = d_rel_bias[l].astype(F32)
        od = _attn_d(qd, kd, vd, knd, _na_bias_tables(rel_bias, S // GRID_W),
                     jnp.max(rel_bias, axis=(1, 2)) * LOG2E)
        xT = _post(xT, oa, ob, oc, od, col(g_out_a[l]), col(g_out_b[l]), col(g_out_d[l]),
                   wt(w_out[l]), col(norm_mix_post[l]), col(norm_mlp_pre[l]), wt(w_up[l]),
                   wt(w_down[l]), col(norm_mlp_post[l]), token_major_out=(l == depth - 1))
    return xT
```

```python
import functools
import math

import numpy as np
import jax
import jax.numpy as jnp
from jax import lax
from jax.experimental import pallas as pl
from jax.experimental.pallas import tpu as pltpu

GRID_W = 64
HEAD_DIM = 64
GROUP_W = 256
N_KV_A = 2
MLA_Q_RANK = 192
MLA_KV_RANK = 128
MLA_NOPE = 64
MLA_ROPE = 32
DIFF_QK = 32
NA_ROWS = 8
NA_COLS = 16
ROPE_BASE = 10000.0
EPS = 1e-6
LOG2E = math.log2(math.e)
NEG_BIG = -1e30

A_COLS = 512
B_COLS = MLA_Q_RANK + MLA_KV_RANK + MLA_ROPE
C_COLS = 768
D_COLS = 768
OFF_B = A_COLS
OFF_C = OFF_B + B_COLS
OFF_D = OFF_C + C_COLS
IN_COLS = OFF_D + D_COLS

TOKEN_TILE = 512
MLP_TOKEN_TILE = 512
MLP_PART = 512
Q_TILE = 512
Q_SUBTILES = 8
KV_CHUNK = 256
QK_LOOKAHEAD = 2
KV_UNROLL = 16
NA_Q_TILE = 256
NA_SLAB_ROWS = 12
VMEM_LIMIT = 56 * 1024 * 1024

F32 = jnp.float32
BF16 = jnp.bfloat16
LANES = 128


def _dot(a, b):
    return jnp.dot(a, b, preferred_element_type=F32)


def _rms_rows(x, gain):
    gain = jnp.concatenate([gain] * (x.shape[1] // LANES), axis=1)
    ms = jnp.mean(x * x, axis=0, keepdims=True)
    return x * lax.rsqrt(ms + EPS) * gain


def _rope_rows(x, cos, sin):
    n = cos.shape[0]
    x1, x2 = x[:n], x[n:]
    return x1 * cos - x2 * sin, x1 * sin + x2 * cos


def _inproj_kernel(x_ref, gpre_ref, w_ref, aq_ref, ak_ref, cqg_ref, ckvg_ref, wuq_ref, wukv_ref,
                   cr_ref, sr_ref, cc_ref, sc_ref, cs_ref, ss_ref,
                   qa_ref, ka_ref, va_ref, qb_ref, kb_ref, vb_ref,
                   qc_ref, kc_ref, vc_ref, qd_ref, kd_ref, vd_ref,
                   kna_ref, knb_ref, knc_ref, knd_ref, *xt_ref):
    if xt_ref:
        x = x_ref[0].T
        xt_ref[0][0] = x
    else:
        x = x_ref[0]
    h = _rms_rows(x, gpre_ref[...]).astype(BF16)
    cr, sr, cc, sc = cr_ref[...], sr_ref[...], cc_ref[...], sc_ref[...]
    cs, ss = cs_ref[...], ss_ref[...]
    tm = x.shape[1]

    def axial(t):
        r1, r2 = _rope_rows(t[0:32], cr, sr)
        c1, c2 = _rope_rows(t[32:64], cc, sc)
        return r1, r2, c1, c2

    def sumsq(t):
        return jnp.sum(t * t, axis=0, keepdims=True)

    ub = _dot(w_ref[OFF_B:OFF_B + B_COLS, :], h)
    ua = _dot(w_ref[0:A_COLS, :], h)
    uc = _dot(w_ref[OFF_C:OFF_C + C_COLS, :], h)
    ud = _dot(w_ref[OFF_D:OFF_D + D_COLS, :], h)
    cqn = _rms_rows(ub[0:MLA_Q_RANK], cqg_ref[...]).astype(BF16)
    ckvn = _rms_rows(ub[MLA_Q_RANK:MLA_Q_RANK + MLA_KV_RANK], ckvg_ref[...]).astype(BF16)
    qb = _dot(wuq_ref[...], cqn)
    kvb = _dot(wukv_ref[...], ckvn)

    qscale = HEAD_DIM ** -0.5 * LOG2E
    for hd in range(4):
        t = _rms_rows(ua[64 * hd:64 * hd + 64], aq_ref[...])
        for j, piece in enumerate(axial(t)):
            qa_ref[0, 64 * hd + 16 * j:64 * hd + 16 * j + 16, :] = (piece * qscale).astype(BF16)
    kparts = []
    for g in range(N_KV_A):
        t = _rms_rows(ua[256 + 64 * g:256 + 64 * g + 64], ak_ref[...])
        kparts.extend(axial(t))
        kna_ref[0, g] = sumsq(t)
    ka_ref[0] = jnp.concatenate(kparts, axis=0).T.astype(BF16)
    va_ref[0] = ua[384:512].astype(BF16)

    kr1, kr2 = _rope_rows(ub[MLA_Q_RANK + MLA_KV_RANK:B_COLS], cs, ss)
    bscale = (MLA_NOPE + MLA_ROPE) ** -0.5 * LOG2E
    zeros32 = jnp.zeros((32, tm), F32)
    for hd in range(4):
        base = 96 * hd
        q1, q2 = _rope_rows(qb[base + 64:base + 96], cs, ss)
        qb_ref[0, 128 * hd:128 * hd + 64, :] = (qb[base:base + 64] * bscale).astype(BF16)
        qb_ref[0, 128 * hd + 64:128 * hd + 80, :] = (q1 * bscale).astype(BF16)
        qb_ref[0, 128 * hd + 80:128 * hd + 96, :] = (q2 * bscale).astype(BF16)
        qb_ref[0, 128 * hd + 96:128 * hd + 128, :] = jnp.zeros((32, tm), BF16)
        kfull = jnp.concatenate([kvb[128 * hd:128 * hd + 64], kr1, kr2, zeros32], axis=0)
        kb_ref[0, hd] = kfull.T.astype(BF16)
        knb_ref[0, hd] = sumsq(kfull)
        vb_ref[0, 64 * hd:64 * hd + 64, :] = kvb[128 * hd + 64:128 * hd + 128].astype(BF16)

    qc_ref[0] = (uc[0:256] * (DIFF_QK ** -0.5 * LOG2E)).astype(BF16)
    for p in range(2):
        kc_ref[0, p] = uc[256 + 128 * p:256 + 128 * p + 128].T.astype(BF16)
    vc_ref[0] = uc[512:768].astype(BF16)
    for hd in range(4):
        for mp in range(2):
            base = 256 + 64 * hd + 32 * mp
            knc_ref[0, hd, mp:mp + 1, :] = sumsq(uc[base:base + 32])

    qd_ref[0] = (ud[0:256] * (HEAD_DIM ** -0.5 * LOG2E)).astype(BF16)
    for p in range(2):
        kd_ref[0, p] = ud[256 + 128 * p:256 + 128 * p + 128].T.astype(BF16)
    vd_ref[0] = ud[512:768].astype(BF16)
    for hd in range(4):
        knd_ref[0, hd] = sumsq(ud[256 + 64 * hd:256 + 64 * hd + 64])


def _const_spec(shape):
    nd = len(shape)
    return pl.BlockSpec(shape, lambda *_: (0,) * nd, pipeline_mode=pl.Buffered(1))


def _inproj(x, gpre, wT, aq, ak, cqg, ckvg, wuqT, wukvT, tables, token_major):
    B, D, S = (x.shape[0], x.shape[2], x.shape[1]) if token_major else x.shape
    tm = min(TOKEN_TILE, S)
    grid = (B, S // tm)
    tab_spec = pl.BlockSpec((16, tm), lambda b, i: (0, i))
    rows_spec = lambda rows: pl.BlockSpec((1, rows, tm), lambda b, i: (b, 0, i))
    out_shape = (
        jax.ShapeDtypeStruct((B, 256, S), BF16),
        jax.ShapeDtypeStruct((B, S, 128), BF16),
        jax.ShapeDtypeStruct((B, 128, S), BF16),
        jax.ShapeDtypeStruct((B, 512, S), BF16),
        jax.ShapeDtypeStruct((B, 4, S, 128), BF16),
        jax.ShapeDtypeStruct((B, 256, S), BF16),
        jax.ShapeDtypeStruct((B, 256, S), BF16),
        jax.ShapeDtypeStruct((B, 2, S, 128), BF16),
        jax.ShapeDtypeStruct((B, 256, S), BF16),
        jax.ShapeDtypeStruct((B, 256, S), BF16),
        jax.ShapeDtypeStruct((B, 2, S, 128), BF16),
        jax.ShapeDtypeStruct((B, 256, S), BF16),
        jax.ShapeDtypeStruct((B, 2, 1, S), F32),
        jax.ShapeDtypeStruct((B, 4, 1, S), F32),
        jax.ShapeDtypeStruct((B, 4, 2, S), F32),
        jax.ShapeDtypeStruct((B, 4, 1, S), F32),
    )
    kspec = lambda n: pl.BlockSpec((1, n, tm, 128), lambda b, i: (b, 0, i, 0))
    nspec = lambda n, r: pl.BlockSpec((1, n, r, tm), lambda b, i: (b, 0, 0, i))
    out_specs = (
        rows_spec(256), pl.BlockSpec((1, tm, 128), lambda b, i: (b, i, 0)), rows_spec(128),
        rows_spec(512), kspec(4), rows_spec(256),
        rows_spec(256), kspec(2), rows_spec(256),
        rows_spec(256), kspec(2), rows_spec(256),
        nspec(2, 1), nspec(4, 1), nspec(4, 2), nspec(4, 1),
    )
    x_spec = rows_spec(D)
    if token_major:
        x_spec = pl.BlockSpec((1, tm, D), lambda b, i: (b, i, 0))
        out_shape += (jax.ShapeDtypeStruct((B, D, S), F32),)
        out_specs += (rows_spec(D),)
    in_specs = [
        x_spec, _const_spec(gpre.shape), _const_spec(wT.shape),
        _const_spec(aq.shape), _const_spec(ak.shape), _const_spec(cqg.shape), _const_spec(ckvg.shape),
        _const_spec(wuqT.shape), _const_spec(wukvT.shape),
    ] + [tab_spec] * 6
    return pl.pallas_call(
        _inproj_kernel, grid=grid, in_specs=in_specs, out_specs=out_specs, out_shape=out_shape,
        compiler_params=pltpu.CompilerParams(vmem_limit_bytes=VMEM_LIMIT),
        name="inproj",
    )(x, gpre, wT, aq, ak, cqg, ckvg, wuqT, wukvT, *tables)


def _pad_rows(q32, slot, nslots):
    return jnp.concatenate([jnp.where(slot == j, q32, 0.0) for j in range(nslots)], axis=0)


def _softmax_stage(t, cst, m):
    m_new = jnp.maximum(m, jnp.max(t, axis=0, keepdims=True) + cst)
    alpha = jnp.exp2(m - m_new)
    p = jnp.exp2(t - (m_new - cst))
    return m_new, alpha, p


def _fold_rows(p):
    slabs = [p[r:r + 8] for r in range(0, p.shape[0], 8)]
    chains = [functools.reduce(jnp.add, slabs[w::4]) for w in range(min(4, len(slabs)))]
    return functools.reduce(jnp.add, chains)


SHIFT_OK_MIN_SUM = 2.0 ** -64
ZERO_PROB_LOG2 = 140.0


def _attention(nchunks, tq, n_sub, n_maps, shift_of, qk_stage, pv_stage, finalize, window=None):
    dv = pv_stage.rows
    zero_acc = lambda: tuple(jnp.zeros((dv, tq), F32) for _ in range(n_maps))
    zero_sum = lambda: tuple(jnp.zeros((8, tq), F32) for _ in range(n_maps))

    def run_tasks(tasks, accs, sums):
        queue = []
        for idx, (sub, c, j) in enumerate(tasks):
            while len(queue) + idx < min(idx + QK_LOOKAHEAD + 1, len(tasks)):
                ahead_sub, ahead_c, ahead_j = tasks[idx + len(queue)]
                queue.append(qk_stage(ahead_c, ahead_j, ahead_sub))
            tile, cst = queue.pop(0)
            p = jnp.exp2(tile - (shift_of(sub, j) - cst))
            sums[sub][j] = sums[sub][j] + _fold_rows(p)
            accs[sub][j] = accs[sub][j] + pv_stage(c, p.astype(BF16))

    unroll = math.gcd(nchunks, KV_UNROLL)
    accs = [list(zero_acc()) for _ in range(n_sub)]
    sums = [list(zero_sum()) for _ in range(n_sub)]
    if window is not None:
        n_win, first = window
        starts = [first(sub) for sub in range(n_sub)]
        run_tasks([(sub, starts[sub] + w, j) for sub in range(n_sub) for w in range(n_win)
                   for j in range(n_maps)], accs, sums)
    elif unroll == nchunks:
        run_tasks([(sub, c, j) for sub in range(n_sub) for c in range(nchunks)
                   for j in range(n_maps)], accs, sums)
    else:
        for sub in range(n_sub):
            for j in range(n_maps):
                shift_of(sub, j)

            def trip(t, carry, sub=sub):
                acc, ssum = {sub: list(carry[0])}, {sub: list(carry[1])}
                run_tasks([(sub, t * unroll + u, j) for u in range(unroll) for j in range(n_maps)],
                          acc, ssum)
                return tuple(acc[sub]), tuple(ssum[sub])
            done = lax.fori_loop(0, nchunks // unroll, trip, (tuple(accs[sub]), tuple(sums[sub])))
            accs[sub], sums[sub] = list(done[0]), list(done[1])

    lmins = []
    for sub in range(n_sub):
        denom = [jnp.sum(s8, axis=0, keepdims=True) for s8 in sums[sub]]
        finalize(sub, list(zip(accs[sub], denom)))
        lmins += [jnp.min(d) for d in denom]

    @pl.when(jnp.logical_not(functools.reduce(jnp.minimum, lmins) >= SHIFT_OK_MIN_SUM))
    def _():
        for sub in range(n_sub):
            def online_trip(c, carry, sub=sub):
                out = []
                for j, (m, l, acc) in enumerate(carry):
                    tile, cst = qk_stage(c, j, sub)
                    m_new, alpha, p = _softmax_stage(tile, cst, m)
                    out.append((m_new, alpha * l + jnp.sum(p, axis=0, keepdims=True),
                                alpha * acc + pv_stage(c, p.astype(BF16))))
                return tuple(out)

            init = tuple((jnp.full((1, tq), NEG_BIG, F32), jnp.zeros((1, tq), F32), acc0)
                         for acc0 in zero_acc())
            done = lax.fori_loop(0, nchunks, online_trip, init)
            finalize(sub, [(acc, l) for _, l, acc in done])


def _chunk_slice(c, tk):
    return pl.ds(c * tk if isinstance(c, int) else pl.multiple_of(c * tk, tk), tk)


def _make_pv_stage(v_ref, tk):
    def pv_stage(c, p):
        return _dot(v_ref[0, :, _chunk_slice(c, tk)], p)

    pv_stage.rows = v_ref.shape[1]
    return pv_stage


BOUND_SLACK = 1.01


def _score_bound(q32, kn2):
    return BOUND_SLACK * jnp.sqrt(jnp.sum(q32 * q32, axis=0, keepdims=True)
                                  * jnp.max(kn2, axis=1, keepdims=True))


def _attn_kernel(q_ref, k_ref, v_ref, kn_ref, o_ref, *, k_slots, nchunks, tk, tq):
    n_sub = q_ref.shape[2] // tq

    @functools.cache
    def queries(sub):
        q32 = q_ref[0, :, sub * tq:(sub + 1) * tq].astype(F32)
        shift = _score_bound(q32, kn_ref[0, 0])
        if k_slots > 1:
            q32 = _pad_rows(q32, pl.program_id(1) // 2, k_slots)
        return q32.astype(BF16), shift

    k_at = (lambda sl: k_ref[0, sl, :]) if len(k_ref.shape) == 3 else (lambda sl: k_ref[0, 0, sl, :])

    def qk_stage(c, j, sub):
        return _dot(k_at(_chunk_slice(c, tk)), queries(sub)[0]), jnp.float32(0.0)

    def finalize(sub, acc):
        (o, l), = acc
        o_ref[0, :, sub * tq:(sub + 1) * tq] = o / l

    _attention(nchunks, tq, n_sub, 1, lambda sub, j: queries(sub)[1], qk_stage,
               _make_pv_stage(v_ref, tk), finalize)


def _attn_a(qT, k, vT, kn):
    B, _, S = qT.shape
    tq, tk = min(Q_TILE, S), min(KV_CHUNK, S)
    tstep = tq * math.gcd(S // tq, Q_SUBTILES)
    return pl.pallas_call(
        functools.partial(_attn_kernel, k_slots=2, nchunks=S // tk, tk=tk, tq=tq),
        grid=(B, 4, S // tstep),
        in_specs=[pl.BlockSpec((1, 64, tstep), lambda b, h, i: (b, h, i)),
                  pl.BlockSpec((1, S, 128), lambda b, h, i: (b, 0, 0)),
                  pl.BlockSpec((1, 64, S), lambda b, h, i: (b, h // 2, 0)),
                  pl.BlockSpec((1, 1, 1, S), lambda b, h, i: (b, h // 2, 0, 0))],
        out_specs=pl.BlockSpec((1, 64, tstep), lambda b, h, i: (b, h, i)),
        out_shape=jax.ShapeDtypeStruct((B, 256, S), F32),
        compiler_params=pltpu.CompilerParams(vmem_limit_bytes=VMEM_LIMIT),
        name="attn_gqa",
    )(qT, k, vT, kn)


def _attn_b(qT, k, vT, kn):
    B, _, S = qT.shape
    tq, tk = min(Q_TILE, S), min(KV_CHUNK, S)
    tstep = tq * math.gcd(S // tq, Q_SUBTILES)
    return pl.pallas_call(
        functools.partial(_attn_kernel, k_slots=1, nchunks=S // tk, tk=tk, tq=tq),
        grid=(B, 4, S // tstep),
        in_specs=[pl.BlockSpec((1, 128, tstep), lambda b, h, i: (b, h, i)),
                  pl.BlockSpec((1, 1, S, 128), lambda b, h, i: (b, h, 0, 0)),
                  pl.BlockSpec((1, 64, S), lambda b, h, i: (b, h, 0)),
                  pl.BlockSpec((1, 1, 1, S), lambda b, h, i: (b, h, 0, 0))],
        out_specs=pl.BlockSpec((1, 64, tstep), lambda b, h, i: (b, h, i)),
        out_shape=jax.ShapeDtypeStruct((B, 256, S), F32),
        compiler_params=pltpu.CompilerParams(vmem_limit_bytes=VMEM_LIMIT),
        name="attn_mla",
    )(qT, k, vT, kn)


def _diff_kernel(slope_ref, lam_ref, q_ref, k_ref, v_ref, kn_ref, g_ref, o_ref, bias_ref,
                 *, nchunks, tk, tq, head0, window, out_scale, lambda_init):
    h = head0 + pl.program_id(1)
    i = pl.program_id(2)
    n_sub = q_ref.shape[2] // tq
    dv = v_ref.shape[1]
    n_inside = tq // tk
    slope = slope_ref[h]

    @pl.when(i == 0)
    def _():
        d = (lax.broadcasted_iota(jnp.int32, (tk, tq), 1)
             - lax.broadcasted_iota(jnp.int32, (tk, tq), 0)).astype(F32)
        bias_ref[0] = -d * slope
        bias_ref[1] = d * slope
        for j in range(n_inside):
            bias_ref[2 + j] = -jnp.abs(d - float(j * tk)) * slope

    slot0 = (h % 2) * 2

    @functools.cache
    def queries(sub, mp):
        q32 = q_ref[0, 32 * mp:32 * mp + 32, sub * tq:(sub + 1) * tq].astype(F32)
        shift = _score_bound(q32, kn_ref[0, 0, mp:mp + 1, :])
        return _pad_rows(q32, slot0 + mp, 4).astype(BF16), shift

    def qk_stage(c, mp, sub):
        kc = k_ref[0, 0, _chunk_slice(c, tk), :]
        gap = (i * n_sub + sub) * tq - c * tk
        inside = jnp.logical_and(gap <= 0, gap > -tq)
        bias = bias_ref[jnp.where(gap > 0, 0, jnp.where(inside, 2 + (-gap) // tk, 1))]
        cst = jnp.where(inside, 0.0, -jnp.abs(gap).astype(F32) * slope)
        return _dot(kc, queries(sub, mp)[0]) + bias, cst

    def finalize(sub, acc):
        (a1, l1), (a2, l2) = acc
        lv = lam_ref[...]
        lam = (jnp.exp(jnp.sum(lv[0:1] * lv[1:2], axis=1, keepdims=True))
               - jnp.exp(jnp.sum(lv[2:3] * lv[3:4], axis=1, keepdims=True)) + lambda_init)
        o = a1 / l1 - lam * (a2 / l2)
        o_ref[0, :, sub * tq:(sub + 1) * tq] = _rms_rows(o, g_ref[...]) * out_scale

    chunk_window = None
    if window is not None:
        n_win, reach = window
        first = lambda sub: jnp.clip(((i * n_sub + sub) * tq - reach) // tk, 0, nchunks - n_win)
        chunk_window = (n_win, first)
    _attention(nchunks, tq, n_sub, 2, lambda sub, mp: queries(sub, mp)[1], qk_stage,
               _make_pv_stage(v_ref, tk), finalize, chunk_window)


def _alibi_window(slope2, n_tokens, tq, tk):
    reach = math.ceil(ZERO_PROB_LOG2 / slope2)
    n_win = -(-(tq + 2 * reach) // tk) + 1
    nchunks = n_tokens // tk
    if n_win >= nchunks:
        return None
    for q0 in range(0, n_tokens, tq):
        first = min(max((q0 - reach) // tk, 0), nchunks - n_win)
        lo, hi = max(q0 - reach, 0) // tk, min(q0 + tq - 1 + reach, n_tokens - 1) // tk
        assert first <= lo and hi < first + n_win
    return n_win, reach


def _attn_c(qT, k, vT, kn, slopes2, lam_vecs, g_head, lambda_init, heads, window):
    B, _, S = qT.shape
    tq, tk = min(Q_TILE, S), min(KV_CHUNK, S)
    tstep = tq * math.gcd(S // tq, Q_SUBTILES)
    smem = pl.BlockSpec(memory_space=pltpu.SMEM)
    h0 = heads[0]
    return pl.pallas_call(
        functools.partial(_diff_kernel, nchunks=S // tk, tk=tk, tq=tq, head0=h0, window=window,
                          out_scale=1.0 - lambda_init, lambda_init=lambda_init),
        grid=(B, len(heads), S // tstep),
        in_specs=[smem,
                  pl.BlockSpec((4, 32), lambda b, h, i: (0, 0)),
                  pl.BlockSpec((1, 64, tstep), lambda b, h, i: (b, h0 + h, i)),
                  pl.BlockSpec((1, 1, S, 128), lambda b, h, i: (b, (h0 + h) // 2, 0, 0)),
                  pl.BlockSpec((1, 64, S), lambda b, h, i: (b, h0 + h, 0)),
                  pl.BlockSpec((1, 1, 2, S), lambda b, h, i: (b, h0 + h, 0, 0)),
                  pl.BlockSpec(g_head.shape, lambda b, h, i: (0, 0))],
        out_specs=pl.BlockSpec((1, 64, tstep), lambda b, h, i: (b, h, i)),
        out_shape=jax.ShapeDtypeStruct((B, 64 * len(heads), S), F32),
        scratch_shapes=[pltpu.VMEM((2 + tq // tk, tk, tq), F32)],
        compiler_params=pltpu.CompilerParams(vmem_limit_bytes=VMEM_LIMIT),
        name="attn_diff",
    )(slopes2, lam_vecs, qT, k, vT, kn, g_head)


def _na_kernel(bmax_ref, q_ref, k_ref, v_ref, kn_ref, bias_ref, o_ref, *, n_rows):
    h = pl.program_id(0)
    n_blocks = n_rows // 4
    nk = NA_SLAB_ROWS * GRID_W
    tq = NA_Q_TILE
    dv = v_ref.shape[1]
    ones = jnp.ones((16, nk), BF16)
    kmax2 = jnp.max(kn_ref[0, 0], axis=1, keepdims=True)
    bmax = bmax_ref[h]

    def variant(a):
        return 0 if a == 0 else (2 if a == n_blocks - 1 else 1)

    def slab(a):
        return min(max(4 * a - 4, 0), n_rows - NA_SLAB_ROWS) * GRID_W

    def q_block(a):
        return q_ref[0, :, a * tq:(a + 1) * tq].astype(F32)

    def scores(a):
        qp = _pad_rows(q_block(a), h % 2, 2).astype(BF16)
        return _dot(k_ref[0, 0, slab(a):slab(a) + nk, :], qp) + bias_ref[0, variant(a)]

    def pv(a, p):
        vaug = jnp.concatenate([v_ref[0, :, slab(a):slab(a) + nk], ones], axis=0)
        return _dot(vaug, p)

    lmin = None
    nxt = scores(0)
    for a in range(n_blocks):
        s = nxt
        if a + 1 < n_blocks:
            nxt = scores(a + 1)
        q32 = q_block(a)
        shift = jnp.sqrt(jnp.sum(q32 * q32, axis=0, keepdims=True) * kmax2) + bmax
        acc = pv(a, jnp.exp2(s - shift).astype(BF16))
        o_ref[0, :, a * tq:(a + 1) * tq] = acc[0:dv] / acc[dv:dv + 1]
        amin = jnp.min(acc[dv:dv + 1])
        lmin = amin if lmin is None else jnp.minimum(lmin, amin)

    @pl.when(jnp.logical_not(lmin >= SHIFT_OK_MIN_SUM))
    def _():
        for a in range(n_blocks):
            s = scores(a)
            p = jnp.exp2(s - jnp.max(s, axis=0, keepdims=True)).astype(BF16)
            acc = pv(a, p)
            o_ref[0, :, a * tq:(a + 1) * tq] = acc[0:dv] / acc[dv:dv + 1]


def _na_bias_tables(rel_bias, n_rows):
    n_blocks = n_rows // 4
    kr_w = min(NA_ROWS, n_rows)
    col = np.arange(GRID_W)
    c0 = np.clip(col - NA_COLS // 2, 0, GRID_W - NA_COLS)
    col_ok = (col[:, None] >= c0[None, :]) & (col[:, None] < c0[None, :] + NA_COLS)
    dc = col[:, None] - col[None, :] + (NA_COLS - 1)
    onehot_c = ((dc[None] == np.arange(2 * NA_COLS - 1)[:, None, None]) & col_ok[None]).astype(np.float32)

    def row_geometry(a):
        R0 = int(np.clip(4 * a - 4, 0, n_rows - NA_SLAB_ROWS))
        r = 4 * a + np.arange(4)
        r0 = np.clip(r - kr_w // 2, 0, n_rows - kr_w)
        kr = R0 + np.arange(NA_SLAB_ROWS)
        ok = (kr[:, None] >= r0[None, :]) & (kr[:, None] < r0[None, :] + kr_w)
        dr = kr[:, None] - r[None, :] + (NA_ROWS - 1)
        return ((dr[:, :, None] == np.arange(2 * NA_ROWS - 1)) & ok[:, :, None]), ok

    variants = [row_geometry(0), row_geometry(1), row_geometry(n_blocks - 1)]
    for a in range(1, n_blocks - 1):
        assert np.array_equal(row_geometry(a)[0], variants[1][0])
    onehot_r = np.stack([v[0] for v in variants]).astype(np.float32)
    valid = (np.stack([v[1] for v in variants])[:, :, None, :, None]
             & col_ok[None, None, :, None, :])
    assert (valid.reshape(3, -1, NA_Q_TILE).sum(axis=1) == kr_w * NA_COLS).all()
    hi = lax.Precision.HIGHEST
    t1 = jnp.einsum('hrd,dkc->hrkc', rel_bias, onehot_c, precision=hi)
    tab = jnp.einsum('viqr,hrkc->hvikqc', onehot_r, t1, precision=hi) * LOG2E
    tab = jnp.where(valid[None], tab, NEG_BIG)
    return tab.reshape(4, 3, NA_SLAB_ROWS * GRID_W, NA_Q_TILE).astype(F32)


def _attn_d(qT, k, vT, kn, bias_tab, bias_max):
    B, _, S = qT.shape
    n_rows = S // GRID_W
    nk = NA_SLAB_ROWS * GRID_W
    return pl.pallas_call(
        functools.partial(_na_kernel, n_rows=n_rows),
        grid=(4, B),
        in_specs=[pl.BlockSpec(memory_space=pltpu.SMEM),
                  pl.BlockSpec((1, 64, S), lambda h, b: (b, h, 0)),
                  pl.BlockSpec((1, 1, S, 128), lambda h, b: (b, h // 2, 0, 0)),
                  pl.BlockSpec((1, 64, S), lambda h, b: (b, h, 0)),
                  pl.BlockSpec((1, 1, 1, S), lambda h, b: (b, h, 0, 0)),
                  pl.BlockSpec((1, 3, nk, NA_Q_TILE), lambda h, b: (h, 0, 0, 0))],
        out_specs=pl.BlockSpec((1, 64, S), lambda h, b: (b, h, 0)),
        out_shape=jax.ShapeDtypeStruct((B, 256, S), F32),
        compiler_params=pltpu.CompilerParams(vmem_limit_bytes=VMEM_LIMIT),
        name="attn_nbr",
    )(bias_max, qT, k, vT, kn, bias_tab)


def _post_kernel(x_ref, oa_ref, ob_ref, od_ref, *rest, ff_chunk, n_parts, n_oc, token_major_out):
    oc_refs = rest[:n_oc]
    (ga_ref, gb_ref, gd_ref, wout_ref, gpost_ref, gmlp_ref, wup_ref, wdown_ref, gmlp_post_ref,
     y_ref) = rest[n_oc:]
    tm = x_ref.shape[2]
    parts = [slice(s * tm // n_parts, (s + 1) * tm // n_parts) for s in range(n_parts)]
    n_ff = wup_ref.shape[0] // ff_chunk

    def mixer_out(cols):
        cat = jnp.concatenate([
            _rms_rows(oa_ref[0, :, cols], ga_ref[...]).astype(BF16),
            _rms_rows(ob_ref[0, :, cols], gb_ref[...]).astype(BF16),
            *[oc_ref[0, :, cols].astype(BF16) for oc_ref in oc_refs],
            _rms_rows(od_ref[0, :, cols], gd_ref[...]).astype(BF16)], axis=0)
        return _dot(wout_ref[...], cat)

    def mlp(h):
        f = None
        for j in range(n_ff):
            up = _dot(wup_ref[j * ff_chunk:(j + 1) * ff_chunk, :], h)
            act = jnp.square(jnp.maximum(up, 0.0)).astype(BF16)
            down = _dot(wdown_ref[:, j * ff_chunk:(j + 1) * ff_chunk], act)
            f = down if f is None else f + down
        return f

    mix = [mixer_out(cols) for cols in parts]
    x1, f = [], []
    for s, cols in enumerate(parts):
        x1.append(x_ref[0, :, cols] + _rms_rows(mix[s], gpost_ref[...]))
        f.append(mlp(_rms_rows(x1[s], gmlp_ref[...]).astype(BF16)))
    for s, cols in enumerate(parts):
        y = x1[s] + _rms_rows(f[s], gmlp_post_ref[...])
        if token_major_out:
            y_ref[0, cols, :] = y.T
        else:
            y_ref[0, :, cols] = y


def _post(xT, oa, ob, oc_parts, od, ga, gb, gd, woutT, gpost, gmlp, wupT, wdownT, gmlp_post,
          token_major_out):
    B, D, S = xT.shape
    tm = min(MLP_TOKEN_TILE, S)
    rows_spec = lambda rows: pl.BlockSpec((1, rows, tm), lambda b, i: (b, 0, i))
    consts = (ga, gb, gd, woutT, gpost, gmlp, wupT, wdownT, gmlp_post)
    out_spec, out_dims = rows_spec(D), (B, D, S)
    if token_major_out:
        out_spec, out_dims = pl.BlockSpec((1, tm, D), lambda b, i: (b, i, 0)), (B, S, D)
    return pl.pallas_call(
        functools.partial(_post_kernel, ff_chunk=1024, n_parts=tm // math.gcd(tm, MLP_PART),
                          n_oc=len(oc_parts), token_major_out=token_major_out),
        grid=(B, S // tm),
        in_specs=([rows_spec(D)] + [rows_spec(256)] * 3 + [rows_spec(o.shape[1]) for o in oc_parts]
                  + [_const_spec(c.shape) for c in consts]),
        out_specs=out_spec,
        out_shape=jax.ShapeDtypeStruct(out_dims, F32),
        compiler_params=pltpu.CompilerParams(vmem_limit_bytes=VMEM_LIMIT),
        name="outproj_mlp",
    )(xT, oa, ob, od, *oc_parts, *consts)


def _rope_tables(n_tokens):
    def angles(pos, dim):
        inv = ROPE_BASE ** (-jnp.arange(0, dim, 2, dtype=F32) / dim)
        return pos.astype(F32)[:, None] * inv[None, :]

    t = jnp.arange(n_tokens, dtype=jnp.int32)
    out = []
    for ang in (angles(t // GRID_W, HEAD_DIM // 2), angles(t % GRID_W, HEAD_DIM // 2),
                angles(t, MLA_ROPE)):
        out += [jnp.cos(ang).T, jnp.sin(ang).T]
    return out


def kernel(x, norm_mix_pre, norm_mix_post, norm_mlp_pre, norm_mlp_post, w_in, a_q_norm, a_k_norm,
           b_cq_norm, b_ckv_norm, b_w_uq, b_w_ukv, c_lambda_q1, c_lambda_k1, c_lambda_q2,
           c_lambda_k2, d_rel_bias, g_out_a, g_out_b, g_out_c, g_out_d, w_out, w_up, w_down):
    B, S, D = x.shape
    depth = w_in.shape[0]
    col = lambda v: jnp.broadcast_to(v.astype(F32).reshape(-1, 1), (v.size, LANES))
    wt = lambda w: w.T.astype(BF16)
    tables = _rope_tables(S)
    slopes = [2.0 ** (-8.0 * (hd + 1) / 4) for hd in range(4)]
    slopes2 = jnp.asarray(slopes, F32) * LOG2E
    tq, tk = min(Q_TILE, S), min(KV_CHUNK, S)
    windows = [_alibi_window(sl * LOG2E, S, tq, tk) for sl in slopes]
    head_groups = []
    for hd in range(4):
        if head_groups and windows[hd] is None and head_groups[-1][1] is None:
            head_groups[-1][0].append(hd)
        else:
            head_groups.append(([hd], windows[hd]))
    xT = x
    for l in range(depth):
        (qa, ka, va, qb, kb, vb, qc, kc, vc, qd, kd, vd, kna, knb, knc, knd, *x_feature_major) = _inproj(
            xT, col(norm_mix_pre[l]), wt(w_in[l]), col(a_q_norm[l]), col(a_k_norm[l]),
            col(b_cq_norm[l]), col(b_ckv_norm[l]), wt(b_w_uq[l]), wt(b_w_ukv[l]), tables,
            token_major=(l == 0))
        if l == 0:
            (xT,) = x_feature_major
        lambda_init = 0.8 - 0.6 * math.exp(-0.3 * l)
        lam_vecs = jnp.stack([c_lambda_q1[l], c_lambda_k1[l], c_lambda_q2[l], c_lambda_k2[l]]).astype(F32)
        oa = _attn_a(qa, ka, va, kna)
        ob = _attn_b(qb, kb, vb, knb)
        oc = [_attn_c(qc, kc, vc, knc, slopes2, lam_vecs, col(g_out_c[l]), lambda_init, tuple(hs), win)
              for hs, win in head_groups]
        rel_bias = d_rel_bias[l].astype(F32)
        od = _attn_d(qd, kd, vd, knd, _na_bias_tables(rel_bias, S // GRID_W),
                     jnp.max(rel_bias, axis=(1, 2)) * LOG2E)
        xT = _post(xT, oa, ob, oc, od, col(g_out_a[l]), col(g_out_b[l]), col(g_out_d[l]),
                   wt(w_out[l]), col(norm_mix_post[l]), col(norm_mlp_pre[l]), wt(w_up[l]),
                   wt(w_down[l]), col(norm_mlp_post[l]), token_major_out=(l == depth - 1))
    return xT
```

```python
import functools
import math

import numpy as np
import jax
import jax.numpy as jnp
from jax import lax
from jax.experimental import pallas as pl
from jax.experimental.pallas import tpu as pltpu

GRID_W = 64
HEAD_DIM = 64
GROUP_W = 256
N_KV_A = 2
MLA_Q_RANK = 192
MLA_KV_RANK = 128
MLA_NOPE = 64
MLA_ROPE = 32
DIFF_QK = 32
NA_ROWS = 8
NA_COLS = 16
ROPE_BASE = 10000.0
EPS = 1e-6
LOG2E = math.log2(math.e)
NEG_BIG = -1e30

A_COLS = 512
B_COLS = MLA_Q_RANK + MLA_KV_RANK + MLA_ROPE
C_COLS = 768
D_COLS = 768
OFF_B = A_COLS
OFF_C = OFF_B + B_COLS
OFF_D = OFF_C + C_COLS
IN_COLS = OFF_D + D_COLS

TOKEN_TILE = 512
MLP_TOKEN_TILE = 512
MLP_PART = 512
Q_TILE = 512
Q_SUBTILES = 8
KV_CHUNK = 256
QK_LOOKAHEAD = 2
KV_UNROLL = 16
NA_Q_TILE = 256
NA_SLAB_ROWS = 12
VMEM_LIMIT = 56 * 1024 * 1024

F32 = jnp.float32
BF16 = jnp.bfloat16
LANES = 128


def _dot(a, b):
    return jnp.dot(a, b, preferred_element_type=F32)


def _rms_rows(x, gain):
    gain = jnp.concatenate([gain] * (x.shape[1] // LANES), axis=1)
    ms = jnp.mean(x * x, axis=0, keepdims=True)
    return x * lax.rsqrt(ms + EPS) * gain


def _rope_rows(x, cos, sin):
    n = cos.shape[0]
    x1, x2 = x[:n], x[n:]
    return x1 * cos - x2 * sin, x1 * sin + x2 * cos


def _inproj_kernel(x_ref, gpre_ref, w_ref, aq_ref, ak_ref, cqg_ref, ckvg_ref, wuq_ref, wukv_ref,
                   cr_ref, sr_ref, cc_ref, sc_ref, cs_ref, ss_ref,
                   qa_ref, ka_ref, va_ref, qb_ref, kb_ref, vb_ref,
                   qc_ref, kc_ref, vc_ref, qd_ref, kd_ref, vd_ref,
                   kna_ref, knb_ref, knc_ref, knd_ref, *xt_ref):
    if xt_ref:
        x = x_ref[0].T
        xt_ref[0][0] = x
    else:
        x = x_ref[0]
    h = _rms_rows(x, gpre_ref[...]).astype(BF16)
    cr, sr, cc, sc = cr_ref[...], sr_ref[...], cc_ref[...], sc_ref[...]
    cs, ss = cs_ref[...], ss_ref[...]
    tm = x.shape[1]

    def axial(t):
        r1, r2 = _rope_rows(t[0:32], cr, sr)
        c1, c2 = _rope_rows(t[32:64], cc, sc)
        return r1, r2, c1, c2

    def sumsq(t):
        return jnp.sum(t * t, axis=0, keepdims=True)

    ub = _dot(w_ref[OFF_B:OFF_B + B_COLS, :], h)
    ua = _dot(w_ref[0:A_COLS, :], h)
    uc = _dot(w_ref[OFF_C:OFF_C + C_COLS, :], h)
    ud = _dot(w_ref[OFF_D:OFF_D + D_COLS, :], h)
    cqn = _rms_rows(ub[0:MLA_Q_RANK], cqg_ref[...]).astype(BF16)
    ckvn = _rms_rows(ub[MLA_Q_RANK:MLA_Q_RANK + MLA_KV_RANK], ckvg_ref[...]).astype(BF16)
    qb = _dot(wuq_ref[...], cqn)
    kvb = _dot(wukv_ref[...], ckvn)

    qscale = HEAD_DIM ** -0.5 * LOG2E
    for hd in range(4):
        t = _rms_rows(ua[64 * hd:64 * hd + 64], aq_ref[...])
        for j, piece in enumerate(axial(t)):
            qa_ref[0, 64 * hd + 16 * j:64 * hd + 16 * j + 16, :] = (piece * qscale).astype(BF16)
    kparts = []
    for g in range(N_KV_A):
        t = _rms_rows(ua[256 + 64 * g:256 + 64 * g + 64], ak_ref[...])
        kparts.extend(axial(t))
        kna_ref[0, g] = sumsq(t)
    ka_ref[0] = jnp.concatenate(kparts, axis=0).T.astype(BF16)
    va_ref[0] = ua[384:512].astype(BF16)

    kr1, kr2 = _rope_rows(ub[MLA_Q_RANK + MLA_KV_RANK:B_COLS], cs, ss)
    bscale = (MLA_NOPE + MLA_ROPE) ** -0.5 * LOG2E
    zeros32 = jnp.zeros((32, tm), F32)
    for hd in range(4):
        base = 96 * hd
        q1, q2 = _rope_rows(qb[base + 64:base + 96], cs, ss)
        qb_ref[0, 128 * hd:128 * hd + 64, :] = (qb[base:base + 64] * bscale).astype(BF16)
        qb_ref[0, 128 * hd + 64:128 * hd + 80, :] = (q1 * bscale).astype(BF16)
        qb_ref[0, 128 * hd + 80:128 * hd + 96, :] = (q2 * bscale).astype(BF16)
        qb_ref[0, 128 * hd + 96:128 * hd + 128, :] = jnp.zeros((32, tm), BF16)
        kfull = jnp.concatenate([kvb[128 * hd:128 * hd + 64], kr1, kr2, zeros32], axis=0)
        kb_ref[0, hd] = kfull.T.astype(BF16)
        knb_ref[0, hd] = sumsq(kfull)
        vb_ref[0, 64 * hd:64 * hd + 64, :] = kvb[128 * hd + 64:128 * hd + 128].astype(BF16)

    qc_ref[0] = (uc[0:256] * (DIFF_QK ** -0.5 * LOG2E)).astype(BF16)
    for p in range(2):
        kc_ref[0, p] = uc[256 + 128 * p:256 + 128 * p + 128].T.astype(BF16)
    vc_ref[0] = uc[512:768].astype(BF16)
    for hd in range(4):
        for mp in range(2):
            base = 256 + 64 * hd + 32 * mp
            knc_ref[0, hd, mp:mp + 1, :] = sumsq(uc[base:base + 32])

    qd_ref[0] = (ud[0:256] * (HEAD_DIM ** -0.5 * LOG2E)).astype(BF16)
    for p in range(2):
        kd_ref[0, p] = ud[256 + 128 * p:256 + 128 * p + 128].T.astype(BF16)
    vd_ref[0] = ud[512:768].astype(BF16)
    for hd in range(4):
        knd_ref[0, hd] = sumsq(ud[256 + 64 * hd:256 + 64 * hd + 64])


def _const_spec(shape):
    nd = len(shape)
    return pl.BlockSpec(shape, lambda *_: (0,) * nd, pipeline_mode=pl.Buffered(1))


def _inproj(x, gpre, wT, aq, ak, cqg, ckvg, wuqT, wukvT, tables, token_major):
    B, D, S = (x.shape[0], x.shape[2], x.shape[1]) if token_major else x.shape
    tm = min(TOKEN_TILE, S)
    grid = (B, S // tm)
    tab_spec = pl.BlockSpec((16, tm), lambda b, i: (0, i))
    rows_spec = lambda rows: pl.BlockSpec((1, rows, tm), lambda b, i: (b, 0, i))
    out_shape = (
        jax.ShapeDtypeStruct((B, 256, S), BF16),
        jax.ShapeDtypeStruct((B, S, 128), BF16),
        jax.ShapeDtypeStruct((B, 128, S), BF16),
        jax.ShapeDtypeStruct((B, 512, S), BF16),
        jax.ShapeDtypeStruct((B, 4, S, 128), BF16),
        jax.ShapeDtypeStruct((B, 256, S), BF16),
        jax.ShapeDtypeStruct((B, 256, S), BF16),
        jax.ShapeDtypeStruct((B, 2, S, 128), BF16),
        jax.ShapeDtypeStruct((B, 256, S), BF16),
        jax.ShapeDtypeStruct((B, 256, S), BF16),
        jax.ShapeDtypeStruct((B, 2, S, 128), BF16),
        jax.ShapeDtypeStruct((B, 256, S), BF16),
        jax.ShapeDtypeStruct((B, 2, 1, S), F32),
        jax.ShapeDtypeStruct((B, 4, 1, S), F32),
        jax.ShapeDtypeStruct((B, 4, 2, S), F32),
        jax.ShapeDtypeStruct((B, 4, 1, S), F32),
    )
    kspec = lambda n: pl.BlockSpec((1, n, tm, 128), lambda b, i: (b, 0, i, 0))
    nspec = lambda n, r: pl.BlockSpec((1, n, r, tm), lambda b, i: (b, 0, 0, i))
    out_specs = (
        rows_spec(256), pl.BlockSpec((1, tm, 128), lambda b, i: (b, i, 0)), rows_spec(128),
        rows_spec(512), kspec(4), rows_spec(256),
        rows_spec(256), kspec(2), rows_spec(256),
        rows_spec(256), kspec(2), rows_spec(256),
        nspec(2, 1), nspec(4, 1), nspec(4, 2), nspec(4, 1),
    )
    x_spec = rows_spec(D)
    if token_major:
        x_spec = pl.BlockSpec((1, tm, D), lambda b, i: (b, i, 0))
        out_shape += (jax.ShapeDtypeStruct((B, D, S), F32),)
        out_specs += (rows_spec(D),)
    in_specs = [
        x_spec, _const_spec(gpre.shape), _const_spec(wT.shape),
        _const_spec(aq.shape), _const_spec(ak.shape), _const_spec(cqg.shape), _const_spec(ckvg.shape),
        _const_spec(wuqT.shape), _const_spec(wukvT.shape),
    ] + [tab_spec] * 6
    return pl.pallas_call(
        _inproj_kernel, grid=grid, in_specs=in_specs, out_specs=out_specs, out_shape=out_shape,
        compiler_params=pltpu.CompilerParams(vmem_limit_bytes=VMEM_LIMIT),
        name="inproj",
    )(x, gpre, wT, aq, ak, cqg, ckvg, wuqT, wukvT, *tables)


def _pad_rows(q32, slot, nslots):
    return jnp.concatenate([jnp.where(slot == j, q32, 0.0) for j in range(nslots)], axis=0)


def _softmax_stage(t, cst, m):
    m_new = jnp.maximum(m, jnp.max(t, axis=0, keepdims=True) + cst)
    alpha = jnp.exp2(m - m_new)
    p = jnp.exp2(t - (m_new - cst))
    return m_new, alpha, p


def _fold_rows(p):
    slabs = [p[r:r + 8] for r in range(0, p.shape[0], 8)]
    chains = [functools.reduce(jnp.add, slabs[w::4]) for w in range(min(4, len(slabs)))]
    return functools.reduce(jnp.add, chains)


SHIFT_OK_MIN_SUM = 2.0 ** -64
ZERO_PROB_LOG2 = 140.0


def _attention(nchunks, tq, n_sub, n_maps, shift_of, qk_stage, pv_stage, finalize, window=None):
    dv = pv_stage.rows
    zero_acc = lambda: tuple(jnp.zeros((dv, tq), F32) for _ in range(n_maps))
    zero_sum = lambda: tuple(jnp.zeros((8, tq), F32) for _ in range(n_maps))

    def run_tasks(tasks, accs, sums):
        queue = []
        for idx, (sub, c, j) in enumerate(tasks):
            while len(queue) + idx < min(idx + QK_LOOKAHEAD + 1, len(tasks)):
                ahead_sub, ahead_c, ahead_j = tasks[idx + len(queue)]
                queue.append(qk_stage(ahead_c, ahead_j, ahead_sub))
            tile, cst = queue.pop(0)
            p = jnp.exp2(tile - (shift_of(sub, j) - cst))
            sums[sub][j] = sums[sub][j] + _fold_rows(p)
            accs[sub][j] = accs[sub][j] + pv_stage(c, p.astype(BF16))

    unroll = math.gcd(nchunks, KV_UNROLL)
    accs = [list(zero_acc()) for _ in range(n_sub)]
    sums = [list(zero_sum()) for _ in range(n_sub)]
    if window is not None:
        run_tasks([(sub, window[sub][0] + w, j) for sub in range(n_sub) for w in range(window[sub][1])
                   for j in range(n_maps)], accs, sums)
    elif unroll == nchunks:
        run_tasks([(sub, c, j) for sub in range(n_sub) for c in range(nchunks)
                   for j in range(n_maps)], accs, sums)
    else:
        for sub in range(n_sub):
            for j in range(n_maps):
                shift_of(sub, j)

            def trip(t, carry, sub=sub):
                acc, ssum = {sub: list(carry[0])}, {sub: list(carry[1])}
                run_tasks([(sub, t * unroll + u, j) for u in range(unroll) for j in range(n_maps)],
                          acc, ssum)
                return tuple(acc[sub]), tuple(ssum[sub])
            done = lax.fori_loop(0, nchunks // unroll, trip, (tuple(accs[sub]), tuple(sums[sub])))
            accs[sub], sums[sub] = list(done[0]), list(done[1])

    lmins = []
    for sub in range(n_sub):
        denom = [jnp.sum(s8, axis=0, keepdims=True) for s8 in sums[sub]]
        finalize(sub, list(zip(accs[sub], denom)))
        lmins += [jnp.min(d) for d in denom]

    @pl.when(jnp.logical_not(functools.reduce(jnp.minimum, lmins) >= SHIFT_OK_MIN_SUM))
    def _():
        for sub in range(n_sub):
            def online_trip(c, carry, sub=sub):
                out = []
                for j, (m, l, acc) in enumerate(carry):
                    tile, cst = qk_stage(c, j, sub)
                    m_new, alpha, p = _softmax_stage(tile, cst, m)
                    out.append((m_new, alpha * l + jnp.sum(p, axis=0, keepdims=True),
                                alpha * acc + pv_stage(c, p.astype(BF16))))
                return tuple(out)

            init = tuple((jnp.full((1, tq), NEG_BIG, F32), jnp.zeros((1, tq), F32), acc0)
                         for acc0 in zero_acc())
            done = lax.fori_loop(0, nchunks, online_trip, init)
            finalize(sub, [(acc, l) for _, l, acc in done])


def _chunk_slice(c, tk):
    return pl.ds(c * tk if isinstance(c, int) else pl.multiple_of(c * tk, tk), tk)


def _make_pv_stage(v_ref, tk):
    def pv_stage(c, p):
        return _dot(v_ref[0, :, _chunk_slice(c, tk)], p)

    pv_stage.rows = v_ref.shape[1]
    return pv_stage


BOUND_SLACK = 1.01


def _score_bound(q32, kn2):
    return BOUND_SLACK * jnp.sqrt(jnp.sum(q32 * q32, axis=0, keepdims=True)
                                  * jnp.max(kn2, axis=1, keepdims=True))


def _attn_kernel(q_ref, k_ref, v_ref, kn_ref, o_ref, *, k_slots, nchunks, tk, tq):
    n_sub = q_ref.shape[2] // tq

    @functools.cache
    def queries(sub):
        q32 = q_ref[0, :, sub * tq:(sub + 1) * tq].astype(F32)
        shift = _score_bound(q32, kn_ref[0, 0])
        if k_slots > 1:
            q32 = _pad_rows(q32, pl.program_id(1) // 2, k_slots)
        return q32.astype(BF16), shift

    k_at = (lambda sl: k_ref[0, sl, :]) if len(k_ref.shape) == 3 else (lambda sl: k_ref[0, 0, sl, :])

    def qk_stage(c, j, sub):
        return _dot(k_at(_chunk_slice(c, tk)), queries(sub)[0]), jnp.float32(0.0)

    def finalize(sub, acc):
        (o, l), = acc
        o_ref[0, :, sub * tq:(sub + 1) * tq] = o / l

    _attention(nchunks, tq, n_sub, 1, lambda sub, j: queries(sub)[1], qk_stage,
               _make_pv_stage(v_ref, tk), finalize)


def _attn_a(qT, k, vT, kn):
    B, _, S = qT.shape
    tq, tk = min(Q_TILE, S), min(KV_CHUNK, S)
    tstep = tq * math.gcd(S // tq, Q_SUBTILES)
    return pl.pallas_call(
        functools.partial(_attn_kernel, k_slots=2, nchunks=S // tk, tk=tk, tq=tq),
        grid=(B, 4, S // tstep),
        in_specs=[pl.BlockSpec((1, 64, tstep), lambda b, h, i: (b, h, i)),
                  pl.BlockSpec((1, S, 128), lambda b, h, i: (b, 0, 0)),
                  pl.BlockSpec((1, 64, S), lambda b, h, i: (b, h // 2, 0)),
                  pl.BlockSpec((1, 1, 1, S), lambda b, h, i: (b, h // 2, 0, 0))],
        out_specs=pl.BlockSpec((1, 64, tstep), lambda b, h, i: (b, h, i)),
        out_shape=jax.ShapeDtypeStruct((B, 256, S), F32),
        compiler_params=pltpu.CompilerParams(vmem_limit_bytes=VMEM_LIMIT),
        name="attn_gqa",
    )(qT, k, vT, kn)


def _attn_b(qT, k, vT, kn):
    B, _, S = qT.shape
    tq, tk = min(Q_TILE, S), min(KV_CHUNK, S)
    tstep = tq * math.gcd(S // tq, Q_SUBTILES)
    return pl.pallas_call(
        functools.partial(_attn_kernel, k_slots=1, nchunks=S // tk, tk=tk, tq=tq),
        grid=(B, 4, S // tstep),
        in_specs=[pl.BlockSpec((1, 128, tstep), lambda b, h, i: (b, h, i)),
                  pl.BlockSpec((1, 1, S, 128), lambda b, h, i: (b, h, 0, 0)),
                  pl.BlockSpec((1, 64, S), lambda b, h, i: (b, h, 0)),
                  pl.BlockSpec((1, 1, 1, S), lambda b, h, i: (b, h, 0, 0))],
        out_specs=pl.BlockSpec((1, 64, tstep), lambda b, h, i: (b, h, i)),
        out_shape=jax.ShapeDtypeStruct((B, 256, S), F32),
        compiler_params=pltpu.CompilerParams(vmem_limit_bytes=VMEM_LIMIT),
        name="attn_mla",
    )(qT, k, vT, kn)


def _diff_kernel(slope_ref, lam_ref, q_ref, k_ref, v_ref, kn_ref, g_ref, o_ref, bias_ref,
                 *, nchunks, tk, tq, head0, reach, out_scale, lambda_init):
    h = head0 + pl.program_id(1)
    n_sub = q_ref.shape[2] // tq
    one_step = n_sub * tq == nchunks * tk
    i = 0 if one_step else pl.program_id(2)
    dv = v_ref.shape[1]
    n_inside = tq // tk
    slope = slope_ref[h]

    @pl.when(pl.program_id(2) == 0)
    def _():
        d = (lax.broadcasted_iota(jnp.int32, (tk, tq), 1)
             - lax.broadcasted_iota(jnp.int32, (tk, tq), 0)).astype(F32)
        bias_ref[0] = -d * slope
        bias_ref[1] = d * slope
        for j in range(n_inside):
            bias_ref[2 + j] = -jnp.abs(d - float(j * tk)) * slope

    slot0 = (h % 2) * 2

    @functools.cache
    def queries(sub, mp):
        q32 = q_ref[0, 32 * mp:32 * mp + 32, sub * tq:(sub + 1) * tq].astype(F32)
        shift = _score_bound(q32, kn_ref[0, 0, mp:mp + 1, :])
        return _pad_rows(q32, slot0 + mp, 4).astype(BF16), shift

    def alibi_case(gap):
        if isinstance(gap, int):
            if gap > 0:
                return 0, -gap * slope
            return (2 + (-gap) // tk, jnp.float32(0.0)) if gap > -tq else (1, gap * slope)
        inside = jnp.logical_and(gap <= 0, gap > -tq)
        return (jnp.where(gap > 0, 0, jnp.where(inside, 2 + (-gap) // tk, 1)),
                jnp.where(inside, 0.0, -jnp.abs(gap).astype(F32) * slope))

    def qk_stage(c, mp, sub):
        kc = k_ref[0, 0, _chunk_slice(c, tk), :]
        table, cst = alibi_case((i * n_sub + sub) * tq - c * tk)
        return _dot(kc, queries(sub, mp)[0]) + bias_ref[table], cst

    def finalize(sub, acc):
        (a1, l1), (a2, l2) = acc
        lv = lam_ref[...]
        lam = (jnp.exp(jnp.sum(lv[0:1] * lv[1:2], axis=1, keepdims=True))
               - jnp.exp(jnp.sum(lv[2:3] * lv[3:4], axis=1, keepdims=True)) + lambda_init)
        o = a1 / l1 - lam * (a2 / l2)
        o_ref[0, :, sub * tq:(sub + 1) * tq] = _rms_rows(o, g_ref[...]) * out_scale

    window = None
    if reach is not None:
        window = _alibi_windows(reach, nchunks, tk, tq, n_sub, None if one_step else i)
    _attention(nchunks, tq, n_sub, 2, lambda sub, mp: queries(sub, mp)[1], qk_stage,
               _make_pv_stage(v_ref, tk), finalize, window)


def _alibi_windows(reach, nchunks, tk, tq, n_sub, step):
    if step is None:
        out = []
        for sub in range(n_sub):
            lo = max(sub * tq - reach, 0) // tk
            hi = min(sub * tq + tq - 1 + reach, nchunks * tk - 1) // tk
            out.append((lo, hi - lo + 1))
        return out
    n_win = min(-(-(tq + 2 * reach) // tk) + 1, nchunks)
    for q0 in range(0, nchunks * tk, tq):
        first = min(max((q0 - reach) // tk, 0), nchunks - n_win)
        lo, hi = max(q0 - reach, 0) // tk, min(q0 + tq - 1 + reach, nchunks * tk - 1) // tk
        assert first <= lo and hi < first + n_win
    return [(jnp.clip(((step * n_sub + sub) * tq - reach) // tk, 0, nchunks - n_win), n_win)
            for sub in range(n_sub)]


def _alibi_reach(slope2, n_tokens, tq, tk, tstep):
    reach = math.ceil(ZERO_PROB_LOG2 / slope2)
    nchunks = n_tokens // tk
    if tstep == n_tokens:
        counts = [n for _, n in _alibi_windows(reach, nchunks, tk, tq, n_tokens // tq, None)]
    else:
        counts = [min(-(-(tq + 2 * reach) // tk) + 1, nchunks)]
    return reach if min(counts) < nchunks else None


def _attn_c(qT, k, vT, kn, slopes2, lam_vecs, g_head, lambda_init, heads, reach):
    B, _, S = qT.shape
    tq, tk = min(Q_TILE, S), min(KV_CHUNK, S)
    tstep = tq * math.gcd(S // tq, Q_SUBTILES)
    smem = pl.BlockSpec(memory_space=pltpu.SMEM)
    h0 = heads[0]
    return pl.pallas_call(
        functools.partial(_diff_kernel, nchunks=S // tk, tk=tk, tq=tq, head0=h0, reach=reach,
                          out_scale=1.0 - lambda_init, lambda_init=lambda_init),
        grid=(B, len(heads), S // tstep),
        in_specs=[smem,
                  pl.BlockSpec((4, 32), lambda b, h, i: (0, 0)),
                  pl.BlockSpec((1, 64, tstep), lambda b, h, i: (b, h0 + h, i)),
                  pl.BlockSpec((1, 1, S, 128), lambda b, h, i: (b, (h0 + h) // 2, 0, 0)),
                  pl.BlockSpec((1, 64, S), lambda b, h, i: (b, h0 + h, 0)),
                  pl.BlockSpec((1, 1, 2, S), lambda b, h, i: (b, h0 + h, 0, 0)),
                  pl.BlockSpec(g_head.shape, lambda b, h, i: (0, 0))],
        out_specs=pl.BlockSpec((1, 64, tstep), lambda b, h, i: (b, h, i)),
        out_shape=jax.ShapeDtypeStruct((B, 64 * len(heads), S), F32),
        scratch_shapes=[pltpu.VMEM((2 + tq // tk, tk, tq), F32)],
        compiler_params=pltpu.CompilerParams(vmem_limit_bytes=VMEM_LIMIT),
        name="attn_diff",
    )(slopes2, lam_vecs, qT, k, vT, kn, g_head)


def _na_kernel(bmax_ref, q_ref, k_ref, v_ref, kn_ref, bias_ref, o_ref, *, n_rows):
    h = pl.program_id(0)
    n_blocks = n_rows // 4
    nk = NA_SLAB_ROWS * GRID_W
    tq = NA_Q_TILE
    dv = v_ref.shape[1]
    ones = jnp.ones((16, nk), BF16)
    kmax2 = jnp.max(kn_ref[0, 0], axis=1, keepdims=True)
    bmax = bmax_ref[h]

    def variant(a):
        return 0 if a == 0 else (2 if a == n_blocks - 1 else 1)

    def slab(a):
        return min(max(4 * a - 4, 0), n_rows - NA_SLAB_ROWS) * GRID_W

    def q_block(a):
        return q_ref[0, :, a * tq:(a + 1) * tq].astype(F32)

    def scores(a):
        qp = _pad_rows(q_block(a), h % 2, 2).astype(BF16)
        return _dot(k_ref[0, 0, slab(a):slab(a) + nk, :], qp) + bias_ref[0, variant(a)]

    def pv(a, p):
        vaug = jnp.concatenate([v_ref[0, :, slab(a):slab(a) + nk], ones], axis=0)
        return _dot(vaug, p)

    lmin = None
    nxt = scores(0)
    for a in range(n_blocks):
        s = nxt
        if a + 1 < n_blocks:
            nxt = scores(a + 1)
        q32 = q_block(a)
        shift = jnp.sqrt(jnp.sum(q32 * q32, axis=0, keepdims=True) * kmax2) + bmax
        acc = pv(a, jnp.exp2(s - shift).astype(BF16))
        o_ref[0, :, a * tq:(a + 1) * tq] = acc[0:dv] / acc[dv:dv + 1]
        amin = jnp.min(acc[dv:dv + 1])
        lmin = amin if lmin is None else jnp.minimum(lmin, amin)

    @pl.when(jnp.logical_not(lmin >= SHIFT_OK_MIN_SUM))
    def _():
        for a in range(n_blocks):
            s = scores(a)
            p = jnp.exp2(s - jnp.max(s, axis=0, keepdims=True)).astype(BF16)
            acc = pv(a, p)
            o_ref[0, :, a * tq:(a + 1) * tq] = acc[0:dv] / acc[dv:dv + 1]


def _na_bias_tables(rel_bias, n_rows):
    n_blocks = n_rows // 4
    kr_w = min(NA_ROWS, n_rows)
    col = np.arange(GRID_W)
    c0 = np.clip(col - NA_COLS // 2, 0, GRID_W - NA_COLS)
    col_ok = (col[:, None] >= c0[None, :]) & (col[:, None] < c0[None, :] + NA_COLS)
    dc = col[:, None] - col[None, :] + (NA_COLS - 1)
    onehot_c = ((dc[None] == np.arange(2 * NA_COLS - 1)[:, None, None]) & col_ok[None]).astype(np.float32)

    def row_geometry(a):
        R0 = int(np.clip(4 * a - 4, 0, n_rows - NA_SLAB_ROWS))
        r = 4 * a + np.arange(4)
        r0 = np.clip(r - kr_w // 2, 0, n_rows - kr_w)
        kr = R0 + np.arange(NA_SLAB_ROWS)
        ok = (kr[:, None] >= r0[None, :]) & (kr[:, None] < r0[None, :] + kr_w)
        dr = kr[:, None] - r[None, :] + (NA_ROWS - 1)
        return ((dr[:, :, None] == np.arange(2 * NA_ROWS - 1)) & ok[:, :, None]), ok

    variants = [row_geometry(0), row_geometry(1), row_geometry(n_blocks - 1)]
    for a in range(1, n_blocks - 1):
        assert np.array_equal(row_geometry(a)[0], variants[1][0])
    onehot_r = np.stack([v[0] for v in variants]).astype(np.float32)
    valid = (np.stack([v[1] for v in variants])[:, :, None, :, None]
             & col_ok[None, None, :, None, :])
    assert (valid.reshape(3, -1, NA_Q_TILE).sum(axis=1) == kr_w * NA_COLS).all()
    hi = lax.Precision.HIGHEST
    t1 = jnp.einsum('hrd,dkc->hrkc', rel_bias, onehot_c, precision=hi)
    tab = jnp.einsum('viqr,hrkc->hvikqc', onehot_r, t1, precision=hi) * LOG2E
    tab = jnp.where(valid[None], tab, NEG_BIG)
    return tab.reshape(4, 3, NA_SLAB_ROWS * GRID_W, NA_Q_TILE).astype(F32)


def _attn_d(qT, k, vT, kn, bias_tab, bias_max):
    B, _, S = qT.shape
    n_rows = S // GRID_W
    nk = NA_SLAB_ROWS * GRID_W
    return pl.pallas_call(
        functools.partial(_na_kernel, n_rows=n_rows),
        grid=(4, B),
        in_specs=[pl.BlockSpec(memory_space=pltpu.SMEM),
                  pl.BlockSpec((1, 64, S), lambda h, b: (b, h, 0)),
                  pl.BlockSpec((1, 1, S, 128), lambda h, b: (b, h // 2, 0, 0)),
                  pl.BlockSpec((1, 64, S), lambda h, b: (b, h, 0)),
                  pl.BlockSpec((1, 1, 1, S), lambda h, b: (b, h, 0, 0)),
                  pl.BlockSpec((1, 3, nk, NA_Q_TILE), lambda h, b: (h, 0, 0, 0))],
        out_specs=pl.BlockSpec((1, 64, S), lambda h, b: (b, h, 0)),
        out_shape=jax.ShapeDtypeStruct((B, 256, S), F32),
        compiler_params=pltpu.CompilerParams(vmem_limit_bytes=VMEM_LIMIT),
        name="attn_nbr",
    )(bias_max, qT, k, vT, kn, bias_tab)


def _post_kernel(x_ref, oa_ref, ob_ref, od_ref, *rest, ff_chunk, n_parts, n_oc, token_major_out):
    oc_refs = rest[:n_oc]
    (ga_ref, gb_ref, gd_ref, wout_ref, gpost_ref, gmlp_ref, wup_ref, wdown_ref, gmlp_post_ref,
     y_ref) = rest[n_oc:]
    tm = x_ref.shape[2]
    parts = [slice(s * tm // n_parts, (s + 1) * tm // n_parts) for s in range(n_parts)]
    n_ff = wup_ref.shape[0] // ff_chunk

    def mixer_out(cols):
        cat = jnp.concatenate([
            _rms_rows(oa_ref[0, :, cols], ga_ref[...]).astype(BF16),
            _rms_rows(ob_ref[0, :, cols], gb_ref[...]).astype(BF16),
            *[oc_ref[0, :, cols].astype(BF16) for oc_ref in oc_refs],
            _rms_rows(od_ref[0, :, cols], gd_ref[...]).astype(BF16)], axis=0)
        return _dot(wout_ref[...], cat)

    def mlp(h):
        f = None
        for j in range(n_ff):
            up = _dot(wup_ref[j * ff_chunk:(j + 1) * ff_chunk, :], h)
            act = jnp.square(jnp.maximum(up, 0.0)).astype(BF16)
            down = _dot(wdown_ref[:, j * ff_chunk:(j + 1) * ff_chunk], act)
            f = down if f is None else f + down
        return f

    mix = [mixer_out(cols) for cols in parts]
    x1, f = [], []
    for s, cols in enumerate(parts):
        x1.append(x_ref[0, :, cols] + _rms_rows(mix[s], gpost_ref[...]))
        f.append(mlp(_rms_rows(x1[s], gmlp_ref[...]).astype(BF16)))
    for s, cols in enumerate(parts):
        y = x1[s] + _rms_rows(f[s], gmlp_post_ref[...])
        if token_major_out:
            y_ref[0, cols, :] = y.T
        else:
            y_ref[0, :, cols] = y


def _post(xT, oa, ob, oc_parts, od, ga, gb, gd, woutT, gpost, gmlp, wupT, wdownT, gmlp_post,
          token_major_out):
    B, D, S = xT.shape
    tm = min(MLP_TOKEN_TILE, S)
    rows_spec = lambda rows: pl.BlockSpec((1, rows, tm), lambda b, i: (b, 0, i))
    consts = (ga, gb, gd, woutT, gpost, gmlp, wupT, wdownT, gmlp_post)
    out_spec, out_dims = rows_spec(D), (B, D, S)
    if token_major_out:
        out_spec, out_dims = pl.BlockSpec((1, tm, D), lambda b, i: (b, i, 0)), (B, S, D)
    return pl.pallas_call(
        functools.partial(_post_kernel, ff_chunk=1024, n_parts=tm // math.gcd(tm, MLP_PART),
                          n_oc=len(oc_parts), token_major_out=token_major_out),
        grid=(B, S // tm),
        in_specs=([rows_spec(D)] + [rows_spec(256)] * 3 + [rows_spec(o.shape[1]) for o in oc_parts]
                  + [_const_spec(c.shape) for c in consts]),
        out_specs=out_spec,
        out_shape=jax.ShapeDtypeStruct(out_dims, F32),
        compiler_params=pltpu.CompilerParams(vmem_limit_bytes=VMEM_LIMIT),
        name="outproj_mlp",
    )(xT, oa, ob, od, *oc_parts, *consts)


def _rope_tables(n_tokens):
    def angles(pos, dim):
        inv = ROPE_BASE ** (-jnp.arange(0, dim, 2, dtype=F32) / dim)
        return pos.astype(F32)[:, None] * inv[None, :]

    t = jnp.arange(n_tokens, dtype=jnp.int32)
    out = []
    for ang in (angles(t // GRID_W, HEAD_DIM // 2), angles(t % GRID_W, HEAD_DIM // 2),
                angles(t, MLA_ROPE)):
        out += [jnp.cos(ang).T, jnp.sin(ang).T]
    return out


def kernel(x, norm_mix_pre, norm_mix_post, norm_mlp_pre, norm_mlp_post, w_in, a_q_norm, a_k_norm,
           b_cq_norm, b_ckv_norm, b_w_uq, b_w_ukv, c_lambda_q1, c_lambda_k1, c_lambda_q2,
           c_lambda_k2, d_rel_bias, g_out_a, g_out_b, g_out_c, g_out_d, w_out, w_up, w_down):
    B, S, D = x.shape
    depth = w_in.shape[0]
    col = lambda v: jnp.broadcast_to(v.astype(F32).reshape(-1, 1), (v.size, LANES))
    wt = lambda w: w.T.astype(BF16)
    tables = _rope_tables(S)
    slopes = [2.0 ** (-8.0 * (hd + 1) / 4) for hd in range(4)]
    slopes2 = jnp.asarray(slopes, F32) * LOG2E
    tq, tk = min(Q_TILE, S), min(KV_CHUNK, S)
    tstep = tq * math.gcd(S // tq, Q_SUBTILES)
    reaches = [_alibi_reach(sl * LOG2E, S, tq, tk, tstep) for sl in slopes]
    head_groups = []
    for hd in range(4):
        if head_groups and reaches[hd] is None and head_groups[-1][1] is None:
            head_groups[-1][0].append(hd)
        else:
            head_groups.append(([hd], reaches[hd]))
    xT = x
    for l in range(depth):
        (qa, ka, va, qb, kb, vb, qc, kc, vc, qd, kd, vd, kna, knb, knc, knd, *x_feature_major) = _inproj(
            xT, col(norm_mix_pre[l]), wt(w_in[l]), col(a_q_norm[l]), col(a_k_norm[l]),
            col(b_cq_norm[l]), col(b_ckv_norm[l]), wt(b_w_uq[l]), wt(b_w_ukv[l]), tables,
            token_major=(l == 0))
        if l == 0:
            (xT,) = x_feature_major
        lambda_init = 0.8 - 0.6 * math.exp(-0.3 * l)
        lam_vecs = jnp.stack([c_lambda_q1[l], c_lambda_k1[l], c_lambda_q2[l], c_lambda_k2[l]]).astype(F32)
        oa = _attn_a(qa, ka, va, kna)
        ob = _attn_b(qb, kb, vb, knb)
        oc = [_attn_c(qc, kc, vc, knc, slopes2, lam_vecs, col(g_out_c[l]), lambda_init, tuple(hs), win)
              for hs, win in head_groups]
        rel_bias = d_rel_bias[l].astype(F32)
        od = _attn_d(qd, kd, vd, knd, _na_bias_tables(rel_bias, S // GRID_W),
                     jnp.max(rel_bias, axis=(1, 2)) * LOG2E)
        xT = _post(xT, oa, ob, oc, od, col(g_out_a[l]), col(g_out_b[l]), col(g_out_d[l]),
                   wt(w_out[l]), col(norm_mix_post[l]), col(norm_mlp_pre[l]), wt(w_up[l]),
                   wt(w_down[l]), col(norm_mlp_post[l]), token_major_out=(l == depth - 1))
    return xT
```

```python
import functools
import math

import numpy as np
import jax
import jax.numpy as jnp
from jax import lax
from jax.experimental import pallas as pl
from jax.experimental.pallas import tpu as pltpu

GRID_W = 64
HEAD_DIM = 64
N_KV_A = 2
MLA_Q_RANK = 192
MLA_KV_RANK = 128
MLA_NOPE = 64
MLA_ROPE = 32
DIFF_QK = 32
NA_ROWS = 8
NA_COLS = 16
ROPE_BASE = 10000.0
EPS = 1e-6
LOG2E = math.log2(math.e)
NEG_BIG = -1e30

A_COLS = 512
B_COLS = MLA_Q_RANK + MLA_KV_RANK + MLA_ROPE
C_COLS = 768
D_COLS = 768
OFF_B = A_COLS
OFF_C = OFF_B + B_COLS
OFF_D = OFF_C + C_COLS

TOKEN_TILE = 512
MLP_TOKEN_TILE = 512
MLP_FF_CHUNK = 1024
Q_TILE = 512
Q_SUBTILES = 8
KV_CHUNK = 256
QK_LOOKAHEAD = 2
KV_UNROLL = 16
NA_Q_TILE = 256
NA_SLAB_ROWS = 12
NA_LOOKAHEAD = 2
VMEM_LIMIT = 56 * 1024 * 1024

F32 = jnp.float32
BF16 = jnp.bfloat16
LANES = 128


def _dot(a, b):
    return jnp.dot(a, b, preferred_element_type=F32)


def _rms_rows(x, gain):
    gain = jnp.concatenate([gain] * (x.shape[1] // LANES), axis=1)
    ms = jnp.mean(x * x, axis=0, keepdims=True)
    return x * lax.rsqrt(ms + EPS) * gain


def _rope_rows(x, cos, sin):
    n = cos.shape[0]
    x1, x2 = x[:n], x[n:]
    return x1 * cos - x2 * sin, x1 * sin + x2 * cos


def _inproj_kernel(x_ref, gpre_ref, w_ref, aq_ref, ak_ref, cqg_ref, ckvg_ref, wuq_ref, wukv_ref,
                   cr_ref, sr_ref, cc_ref, sc_ref, cs_ref, ss_ref,
                   qa_ref, ka_ref, va_ref, qb_ref, kb_ref, vb_ref,
                   qc_ref, kc_ref, vc_ref, qd_ref, kd_ref, vd_ref,
                   kna_ref, knb_ref, knc_ref, knd_ref, *xt_ref):
    if xt_ref:
        x = x_ref[0].T
        xt_ref[0][0] = x
    else:
        x = x_ref[0]
    h = _rms_rows(x, gpre_ref[...]).astype(BF16)
    cr, sr, cc, sc = cr_ref[...], sr_ref[...], cc_ref[...], sc_ref[...]
    cs, ss = cs_ref[...], ss_ref[...]
    tm = x.shape[1]

    def axial(t):
        r1, r2 = _rope_rows(t[0:32], cr, sr)
        c1, c2 = _rope_rows(t[32:64], cc, sc)
        return r1, r2, c1, c2

    def sumsq(t):
        return jnp.sum(t * t, axis=0, keepdims=True)

    ub = _dot(w_ref[OFF_B:OFF_B + B_COLS, :], h)
    ua = _dot(w_ref[0:A_COLS, :], h)
    uc = _dot(w_ref[OFF_C:OFF_C + C_COLS, :], h)
    ud = _dot(w_ref[OFF_D:OFF_D + D_COLS, :], h)
    cqn = _rms_rows(ub[0:MLA_Q_RANK], cqg_ref[...]).astype(BF16)
    ckvn = _rms_rows(ub[MLA_Q_RANK:MLA_Q_RANK + MLA_KV_RANK], ckvg_ref[...]).astype(BF16)
    qb = _dot(wuq_ref[...], cqn)
    kvb = _dot(wukv_ref[...], ckvn)

    qscale = HEAD_DIM ** -0.5 * LOG2E
    for hd in range(4):
        t = _rms_rows(ua[64 * hd:64 * hd + 64], aq_ref[...])
        for j, piece in enumerate(axial(t)):
            qa_ref[0, 64 * hd + 16 * j:64 * hd + 16 * j + 16, :] = (piece * qscale).astype(BF16)
    kparts = []
    for g in range(N_KV_A):
        t = _rms_rows(ua[256 + 64 * g:256 + 64 * g + 64], ak_ref[...])
        kparts.extend(axial(t))
        kna_ref[0, g] = sumsq(t)
    ka_ref[0] = jnp.concatenate(kparts, axis=0).T.astype(BF16)
    va_ref[0] = ua[384:512].astype(BF16)

    kr1, kr2 = _rope_rows(ub[MLA_Q_RANK + MLA_KV_RANK:B_COLS], cs, ss)
    bscale = (MLA_NOPE + MLA_ROPE) ** -0.5 * LOG2E
    zeros32 = jnp.zeros((32, tm), F32)
    for hd in range(4):
        base = 96 * hd
        q1, q2 = _rope_rows(qb[base + 64:base + 96], cs, ss)
        qb_ref[0, 128 * hd:128 * hd + 64, :] = (qb[base:base + 64] * bscale).astype(BF16)
        qb_ref[0, 128 * hd + 64:128 * hd + 80, :] = (q1 * bscale).astype(BF16)
        qb_ref[0, 128 * hd + 80:128 * hd + 96, :] = (q2 * bscale).astype(BF16)
        qb_ref[0, 128 * hd + 96:128 * hd + 128, :] = jnp.zeros((32, tm), BF16)
        kfull = jnp.concatenate([kvb[128 * hd:128 * hd + 64], kr1, kr2, zeros32], axis=0)
        kb_ref[0, hd] = kfull.T.astype(BF16)
        knb_ref[0, hd] = sumsq(kfull)
        vb_ref[0, 64 * hd:64 * hd + 64, :] = kvb[128 * hd + 64:128 * hd + 128].astype(BF16)

    qc_ref[0] = (uc[0:256] * (DIFF_QK ** -0.5 * LOG2E)).astype(BF16)
    for p in range(2):
        kc_ref[0, p] = uc[256 + 128 * p:256 + 128 * p + 128].T.astype(BF16)
    vc_ref[0] = uc[512:768].astype(BF16)
    for hd in range(4):
        for mp in range(2):
            base = 256 + 64 * hd + 32 * mp
            knc_ref[0, hd, mp:mp + 1, :] = sumsq(uc[base:base + 32])

    qd_ref[0] = (ud[0:256] * (HEAD_DIM ** -0.5 * LOG2E)).astype(BF16)
    for p in range(2):
        kd_ref[0, p] = ud[256 + 128 * p:256 + 128 * p + 128].T.astype(BF16)
    vd_ref[0] = ud[512:768].astype(BF16)
    for hd in range(4):
        knd_ref[0, hd] = sumsq(ud[256 + 64 * hd:256 + 64 * hd + 64])


def _const_spec(shape):
    nd = len(shape)
    return pl.BlockSpec(shape, lambda *_: (0,) * nd, pipeline_mode=pl.Buffered(1))


def _inproj(x, gpre, wT, aq, ak, cqg, ckvg, wuqT, wukvT, tables, token_major):
    B, D, S = (x.shape[0], x.shape[2], x.shape[1]) if token_major else x.shape
    tm = min(TOKEN_TILE, S)
    grid = (B, S // tm)
    tab_spec = pl.BlockSpec((16, tm), lambda b, i: (0, i))
    rows_spec = lambda rows: pl.BlockSpec((1, rows, tm), lambda b, i: (b, 0, i))
    out_shape = (
        jax.ShapeDtypeStruct((B, 256, S), BF16),
        jax.ShapeDtypeStruct((B, S, 128), BF16),
        jax.ShapeDtypeStruct((B, 128, S), BF16),
        jax.ShapeDtypeStruct((B, 512, S), BF16),
        jax.ShapeDtypeStruct((B, 4, S, 128), BF16),
        jax.ShapeDtypeStruct((B, 256, S), BF16),
        jax.ShapeDtypeStruct((B, 256, S), BF16),
        jax.ShapeDtypeStruct((B, 2, S, 128), BF16),
        jax.ShapeDtypeStruct((B, 256, S), BF16),
        jax.ShapeDtypeStruct((B, 256, S), BF16),
        jax.ShapeDtypeStruct((B, 2, S, 128), BF16),
        jax.ShapeDtypeStruct((B, 256, S), BF16),
        jax.ShapeDtypeStruct((B, 2, 1, S), F32),
        jax.ShapeDtypeStruct((B, 4, 1, S), F32),
        jax.ShapeDtypeStruct((B, 4, 2, S), F32),
        jax.ShapeDtypeStruct((B, 4, 1, S), F32),
    )
    kspec = lambda n: pl.BlockSpec((1, n, tm, 128), lambda b, i: (b, 0, i, 0))
    nspec = lambda n, r: pl.BlockSpec((1, n, r, tm), lambda b, i: (b, 0, 0, i))
    out_specs = (
        rows_spec(256), pl.BlockSpec((1, tm, 128), lambda b, i: (b, i, 0)), rows_spec(128),
        rows_spec(512), kspec(4), rows_spec(256),
        rows_spec(256), kspec(2), rows_spec(256),
        rows_spec(256), kspec(2), rows_spec(256),
        nspec(2, 1), nspec(4, 1), nspec(4, 2), nspec(4, 1),
    )
    x_spec = rows_spec(D)
    if token_major:
        x_spec = pl.BlockSpec((1, tm, D), lambda b, i: (b, i, 0))
        out_shape += (jax.ShapeDtypeStruct((B, D, S), F32),)
        out_specs += (rows_spec(D),)
    in_specs = [
        x_spec, _const_spec(gpre.shape), _const_spec(wT.shape),
        _const_spec(aq.shape), _const_spec(ak.shape), _const_spec(cqg.shape), _const_spec(ckvg.shape),
        _const_spec(wuqT.shape), _const_spec(wukvT.shape),
    ] + [tab_spec] * 6
    return pl.pallas_call(
        _inproj_kernel, grid=grid, in_specs=in_specs, out_specs=out_specs, out_shape=out_shape,
        compiler_params=pltpu.CompilerParams(vmem_limit_bytes=VMEM_LIMIT),
        name="inproj",
    )(x, gpre, wT, aq, ak, cqg, ckvg, wuqT, wukvT, *tables)


def _pad_rows(q32, slot, nslots):
    return jnp.concatenate([jnp.where(slot == j, q32, 0.0) for j in range(nslots)], axis=0)


def _softmax_stage(t, cst, m):
    m_new = jnp.maximum(m, jnp.max(t, axis=0, keepdims=True) + cst)
    alpha = jnp.exp2(m - m_new)
    p = jnp.exp2(t - (m_new - cst))
    return m_new, alpha, p


def _fold_rows(p):
    slabs = [p[r:r + 8] for r in range(0, p.shape[0], 8)]
    chains = [functools.reduce(jnp.add, slabs[w::4]) for w in range(min(4, len(slabs)))]
    return functools.reduce(jnp.add, chains)


SHIFT_OK_MIN_SUM = 2.0 ** -64
ZERO_PROB_LOG2 = 140.0


def _attention(nchunks, tq, n_sub, n_maps, shift_of, qk_stage, pv_stage, finalize, window=None):
    dv = pv_stage.rows
    zero_acc = lambda: tuple(jnp.zeros((dv, tq), F32) for _ in range(n_maps))
    zero_sum = lambda: tuple(jnp.zeros((8, tq), F32) for _ in range(n_maps))

    def run_tasks(tasks, accs, sums):
        queue = []
        for idx, (sub, c, j) in enumerate(tasks):
            while len(queue) + idx < min(idx + QK_LOOKAHEAD + 1, len(tasks)):
                ahead_sub, ahead_c, ahead_j = tasks[idx + len(queue)]
                queue.append(qk_stage(ahead_c, ahead_j, ahead_sub))
            tile, cst = queue.pop(0)
            p = jnp.exp2(tile - (shift_of(sub, j) - cst))
            sums[sub][j] = sums[sub][j] + _fold_rows(p)
            accs[sub][j] = accs[sub][j] + pv_stage(c, p.astype(BF16))

    unroll = math.gcd(nchunks, KV_UNROLL)
    accs = [list(zero_acc()) for _ in range(n_sub)]
    sums = [list(zero_sum()) for _ in range(n_sub)]
    if window is not None:
        run_tasks([(sub, window[sub][0] + w, j) for sub in range(n_sub) for w in range(window[sub][1])
                   for j in range(n_maps)], accs, sums)
    elif unroll == nchunks:
        run_tasks([(sub, c, j) for sub in range(n_sub) for c in range(nchunks)
                   for j in range(n_maps)], accs, sums)
    else:
        for sub in range(n_sub):
            for j in range(n_maps):
                shift_of(sub, j)

            def trip(t, carry, sub=sub):
                acc, ssum = {sub: list(carry[0])}, {sub: list(carry[1])}
                run_tasks([(sub, t * unroll + u, j) for u in range(unroll) for j in range(n_maps)],
                          acc, ssum)
                return tuple(acc[sub]), tuple(ssum[sub])
            done = lax.fori_loop(0, nchunks // unroll, trip, (tuple(accs[sub]), tuple(sums[sub])))
            accs[sub], sums[sub] = list(done[0]), list(done[1])

    lmins = []
    for sub in range(n_sub):
        denom = [jnp.sum(s8, axis=0, keepdims=True) for s8 in sums[sub]]
        finalize(sub, list(zip(accs[sub], denom)))
        lmins += [jnp.min(d) for d in denom]

    @pl.when(jnp.logical_not(functools.reduce(jnp.minimum, lmins) >= SHIFT_OK_MIN_SUM))
    def _():
        for sub in range(n_sub):
            def online_trip(c, carry, sub=sub):
                out = []
                for j, (m, l, acc) in enumerate(carry):
                    tile, cst = qk_stage(c, j, sub)
                    m_new, alpha, p = _softmax_stage(tile, cst, m)
                    out.append((m_new, alpha * l + jnp.sum(p, axis=0, keepdims=True),
                                alpha * acc + pv_stage(c, p.astype(BF16))))
                return tuple(out)

            init = tuple((jnp.full((1, tq), NEG_BIG, F32), jnp.zeros((1, tq), F32), acc0)
                         for acc0 in zero_acc())
            done = lax.fori_loop(0, nchunks, online_trip, init)
            finalize(sub, [(acc, l) for _, l, acc in done])


def _chunk_slice(c, tk):
    return pl.ds(c * tk if isinstance(c, int) else pl.multiple_of(c * tk, tk), tk)


def _make_pv_stage(v_ref, tk):
    def pv_stage(c, p):
        return _dot(v_ref[0, :, _chunk_slice(c, tk)], p)

    pv_stage.rows = v_ref.shape[1]
    return pv_stage


BOUND_SLACK = 1.01


def _score_bound(q32, kn2):
    return BOUND_SLACK * jnp.sqrt(jnp.sum(q32 * q32, axis=0, keepdims=True)
                                  * jnp.max(kn2, axis=1, keepdims=True))


def _attn_kernel(q_ref, k_ref, v_ref, kn_ref, o_ref, *, k_slots, nchunks, tk, tq):
    n_sub = q_ref.shape[2] // tq

    @functools.cache
    def queries(sub):
        q32 = q_ref[0, :, sub * tq:(sub + 1) * tq].astype(F32)
        shift = _score_bound(q32, kn_ref[0, 0])
        if k_slots > 1:
            q32 = _pad_rows(q32, pl.program_id(1) // 2, k_slots)
        return q32.astype(BF16), shift

    k_at = (lambda sl: k_ref[0, sl, :]) if len(k_ref.shape) == 3 else (lambda sl: k_ref[0, 0, sl, :])

    def qk_stage(c, j, sub):
        return _dot(k_at(_chunk_slice(c, tk)), queries(sub)[0]), jnp.float32(0.0)

    def finalize(sub, acc):
        (o, l), = acc
        o_ref[0, :, sub * tq:(sub + 1) * tq] = o / l

    _attention(nchunks, tq, n_sub, 1, lambda sub, j: queries(sub)[1], qk_stage,
               _make_pv_stage(v_ref, tk), finalize)


def _attn_a(qT, k, vT, kn):
    B, _, S = qT.shape
    tq, tk = min(Q_TILE, S), min(KV_CHUNK, S)
    tstep = tq * math.gcd(S // tq, Q_SUBTILES)
    return pl.pallas_call(
        functools.partial(_attn_kernel, k_slots=2, nchunks=S // tk, tk=tk, tq=tq),
        grid=(B, 4, S // tstep),
        in_specs=[pl.BlockSpec((1, 64, tstep), lambda b, h, i: (b, h, i)),
                  pl.BlockSpec((1, S, 128), lambda b, h, i: (b, 0, 0)),
                  pl.BlockSpec((1, 64, S), lambda b, h, i: (b, h // 2, 0)),
                  pl.BlockSpec((1, 1, 1, S), lambda b, h, i: (b, h // 2, 0, 0))],
        out_specs=pl.BlockSpec((1, 64, tstep), lambda b, h, i: (b, h, i)),
        out_shape=jax.ShapeDtypeStruct((B, 256, S), F32),
        compiler_params=pltpu.CompilerParams(vmem_limit_bytes=VMEM_LIMIT),
        name="attn_gqa",
    )(qT, k, vT, kn)


def _attn_b(qT, k, vT, kn):
    B, _, S = qT.shape
    tq, tk = min(Q_TILE, S), min(KV_CHUNK, S)
    tstep = tq * math.gcd(S // tq, Q_SUBTILES)
    return pl.pallas_call(
        functools.partial(_attn_kernel, k_slots=1, nchunks=S // tk, tk=tk, tq=tq),
        grid=(B, 4, S // tstep),
        in_specs=[pl.BlockSpec((1, 128, tstep), lambda b, h, i: (b, h, i)),
                  pl.BlockSpec((1, 1, S, 128), lambda b, h, i: (b, h, 0, 0)),
                  pl.BlockSpec((1, 64, S), lambda b, h, i: (b, h, 0)),
                  pl.BlockSpec((1, 1, 1, S), lambda b, h, i: (b, h, 0, 0))],
        out_specs=pl.BlockSpec((1, 64, tstep), lambda b, h, i: (b, h, i)),
        out_shape=jax.ShapeDtypeStruct((B, 256, S), F32),
        compiler_params=pltpu.CompilerParams(vmem_limit_bytes=VMEM_LIMIT),
        name="attn_mla",
    )(qT, k, vT, kn)


def _diff_kernel(slope_ref, lam_ref, q_ref, k_ref, v_ref, kn_ref, g_ref, o_ref, bias_ref,
                 *, nchunks, tk, tq, head0, reach, out_scale, lambda_init):
    h = head0 + pl.program_id(1)
    n_sub = q_ref.shape[2] // tq
    one_step = n_sub * tq == nchunks * tk
    i = 0 if one_step else pl.program_id(2)
    dv = v_ref.shape[1]
    n_inside = tq // tk
    slope = slope_ref[h]

    @pl.when(pl.program_id(2) == 0)
    def _():
        d = (lax.broadcasted_iota(jnp.int32, (tk, tq), 1)
             - lax.broadcasted_iota(jnp.int32, (tk, tq), 0)).astype(F32)
        bias_ref[0] = -d * slope
        bias_ref[1] = d * slope
        for j in range(n_inside):
            bias_ref[2 + j] = -jnp.abs(d - float(j * tk)) * slope

    slot0 = (h % 2) * 2

    @functools.cache
    def queries(sub, mp):
        q32 = q_ref[0, 32 * mp:32 * mp + 32, sub * tq:(sub + 1) * tq].astype(F32)
        shift = _score_bound(q32, kn_ref[0, 0, mp:mp + 1, :])
        return _pad_rows(q32, slot0 + mp, 4).astype(BF16), shift

    def alibi_case(gap):
        if isinstance(gap, int):
            if gap > 0:
                return 0, -gap * slope
            return (2 + (-gap) // tk, jnp.float32(0.0)) if gap > -tq else (1, gap * slope)
        inside = jnp.logical_and(gap <= 0, gap > -tq)
        return (jnp.where(gap > 0, 0, jnp.where(inside, 2 + (-gap) // tk, 1)),
                jnp.where(inside, 0.0, -jnp.abs(gap).astype(F32) * slope))

    def qk_stage(c, mp, sub):
        kc = k_ref[0, 0, _chunk_slice(c, tk), :]
        table, cst = alibi_case((i * n_sub + sub) * tq - c * tk)
        return _dot(kc, queries(sub, mp)[0]) + bias_ref[table], cst

    def finalize(sub, acc):
        (a1, l1), (a2, l2) = acc
        lv = lam_ref[...]
        lam = (jnp.exp(jnp.sum(lv[0:1] * lv[1:2], axis=1, keepdims=True))
               - jnp.exp(jnp.sum(lv[2:3] * lv[3:4], axis=1, keepdims=True)) + lambda_init)
        o = a1 / l1 - lam * (a2 / l2)
        o_ref[0, :, sub * tq:(sub + 1) * tq] = _rms_rows(o, g_ref[...]) * out_scale

    window = None
    if reach is not None:
        window = _alibi_windows(reach, nchunks, tk, tq, n_sub, None if one_step else i)
    _attention(nchunks, tq, n_sub, 2, lambda sub, mp: queries(sub, mp)[1], qk_stage,
               _make_pv_stage(v_ref, tk), finalize, window)


def _alibi_windows(reach, nchunks, tk, tq, n_sub, step):
    if step is None:
        out = []
        for sub in range(n_sub):
            lo = max(sub * tq - reach, 0) // tk
            hi = min(sub * tq + tq - 1 + reach, nchunks * tk - 1) // tk
            out.append((lo, hi - lo + 1))
        return out
    n_win = min(-(-(tq + 2 * reach) // tk) + 1, nchunks)
    for q0 in range(0, nchunks * tk, tq):
        first = min(max((q0 - reach) // tk, 0), nchunks - n_win)
        lo, hi = max(q0 - reach, 0) // tk, min(q0 + tq - 1 + reach, nchunks * tk - 1) // tk
        assert first <= lo and hi < first + n_win
    return [(jnp.clip(((step * n_sub + sub) * tq - reach) // tk, 0, nchunks - n_win), n_win)
            for sub in range(n_sub)]


def _alibi_reach(slope2, n_tokens, tq, tk, tstep):
    reach = math.ceil(ZERO_PROB_LOG2 / slope2)
    nchunks = n_tokens // tk
    if tstep == n_tokens:
        counts = [n for _, n in _alibi_windows(reach, nchunks, tk, tq, n_tokens // tq, None)]
    else:
        counts = [min(-(-(tq + 2 * reach) // tk) + 1, nchunks)]
    return reach if min(counts) < nchunks else None


def _attn_c(qT, k, vT, kn, slopes2, lam_vecs, g_head, lambda_init, heads, reach):
    B, _, S = qT.shape
    tq, tk = min(Q_TILE, S), min(KV_CHUNK, S)
    tstep = tq * math.gcd(S // tq, Q_SUBTILES)
    smem = pl.BlockSpec(memory_space=pltpu.SMEM)
    h0 = heads[0]
    return pl.pallas_call(
        functools.partial(_diff_kernel, nchunks=S // tk, tk=tk, tq=tq, head0=h0, reach=reach,
                          out_scale=1.0 - lambda_init, lambda_init=lambda_init),
        grid=(B, len(heads), S // tstep),
        in_specs=[smem,
                  pl.BlockSpec((4, 32), lambda b, h, i: (0, 0)),
                  pl.BlockSpec((1, 64, tstep), lambda b, h, i: (b, h0 + h, i)),
                  pl.BlockSpec((1, 1, S, 128), lambda b, h, i: (b, (h0 + h) // 2, 0, 0)),
                  pl.BlockSpec((1, 64, S), lambda b, h, i: (b, h0 + h, 0)),
                  pl.BlockSpec((1, 1, 2, S), lambda b, h, i: (b, h0 + h, 0, 0)),
                  pl.BlockSpec(g_head.shape, lambda b, h, i: (0, 0))],
        out_specs=pl.BlockSpec((1, 64, tstep), lambda b, h, i: (b, h, i)),
        out_shape=jax.ShapeDtypeStruct((B, 64 * len(heads), S), F32),
        scratch_shapes=[pltpu.VMEM((2 + tq // tk, tk, tq), F32)],
        compiler_params=pltpu.CompilerParams(vmem_limit_bytes=VMEM_LIMIT),
        name="attn_diff",
    )(slopes2, lam_vecs, qT, k, vT, kn, g_head)


def _na_kernel(bmax_ref, q_ref, k_ref, v_ref, kn_ref, bias_ref, o_ref, *, n_rows):
    h = pl.program_id(0)
    n_blocks = n_rows // 4
    nk = NA_SLAB_ROWS * GRID_W
    tq = NA_Q_TILE
    dv = v_ref.shape[1]
    ones = jnp.ones((16, nk), BF16)
    kmax2 = jnp.max(kn_ref[0, 0], axis=1, keepdims=True)
    bmax = bmax_ref[h]

    def variant(a):
        return 0 if a == 0 else (2 if a == n_blocks - 1 else 1)

    def slab(a):
        return min(max(4 * a - 4, 0), n_rows - NA_SLAB_ROWS) * GRID_W

    def q_block(a):
        return q_ref[0, :, a * tq:(a + 1) * tq].astype(F32)

    def scores(a):
        qp = _pad_rows(q_block(a), h % 2, 2).astype(BF16)
        return _dot(k_ref[0, 0, slab(a):slab(a) + nk, :], qp) + bias_ref[0, variant(a)]

    def pv(a, p):
        vaug = jnp.concatenate([v_ref[0, :, slab(a):slab(a) + nk], ones], axis=0)
        return _dot(vaug, p)

    lmin = None
    queue = []
    for a in range(n_blocks):
        while len(queue) + a < min(a + NA_LOOKAHEAD + 1, n_blocks):
            queue.append(scores(a + len(queue)))
        s = queue.pop(0)
        q32 = q_block(a)
        shift = jnp.sqrt(jnp.sum(q32 * q32, axis=0, keepdims=True) * kmax2) + bmax
        acc = pv(a, jnp.exp2(s - shift).astype(BF16))
        o_ref[0, :, a * tq:(a + 1) * tq] = acc[0:dv] / acc[dv:dv + 1]
        amin = jnp.min(acc[dv:dv + 1])
        lmin = amin if lmin is None else jnp.minimum(lmin, amin)

    @pl.when(jnp.logical_not(lmin >= SHIFT_OK_MIN_SUM))
    def _():
        for a in range(n_blocks):
            s = scores(a)
            p = jnp.exp2(s - jnp.max(s, axis=0, keepdims=True)).astype(BF16)
            acc = pv(a, p)
            o_ref[0, :, a * tq:(a + 1) * tq] = acc[0:dv] / acc[dv:dv + 1]


def _na_bias_tables(rel_bias, n_rows):
    n_blocks = n_rows // 4
    kr_w = min(NA_ROWS, n_rows)
    col = np.arange(GRID_W)
    c0 = np.clip(col - NA_COLS // 2, 0, GRID_W - NA_COLS)
    col_ok = (col[:, None] >= c0[None, :]) & (col[:, None] < c0[None, :] + NA_COLS)
    dc = col[:, None] - col[None, :] + (NA_COLS - 1)
    onehot_c = ((dc[None] == np.arange(2 * NA_COLS - 1)[:, None, None]) & col_ok[None]).astype(np.float32)

    def row_geometry(a):
        R0 = int(np.clip(4 * a - 4, 0, n_rows - NA_SLAB_ROWS))
        r = 4 * a + np.arange(4)
        r0 = np.clip(r - kr_w // 2, 0, n_rows - kr_w)
        kr = R0 + np.arange(NA_SLAB_ROWS)
        ok = (kr[:, None] >= r0[None, :]) & (kr[:, None] < r0[None, :] + kr_w)
        dr = kr[:, None] - r[None, :] + (NA_ROWS - 1)
        return ((dr[:, :, None] == np.arange(2 * NA_ROWS - 1)) & ok[:, :, None]), ok

    variants = [row_geometry(0), row_geometry(1), row_geometry(n_blocks - 1)]
    for a in range(1, n_blocks - 1):
        assert np.array_equal(row_geometry(a)[0], variants[1][0])
    onehot_r = np.stack([v[0] for v in variants]).astype(np.float32)
    valid = (np.stack([v[1] for v in variants])[:, :, None, :, None]
             & col_ok[None, None, :, None, :])
    assert (valid.reshape(3, -1, NA_Q_TILE).sum(axis=1) == kr_w * NA_COLS).all()
    hi = lax.Precision.HIGHEST
    t1 = jnp.einsum('hrd,dkc->hrkc', rel_bias, onehot_c, precision=hi)
    tab = jnp.einsum('viqr,hrkc->hvikqc', onehot_r, t1, precision=hi) * LOG2E
    tab = jnp.where(valid[None], tab, NEG_BIG)
    return tab.reshape(4, 3, NA_SLAB_ROWS * GRID_W, NA_Q_TILE).astype(F32)


def _attn_d(qT, k, vT, kn, bias_tab, bias_max):
    B, _, S = qT.shape
    n_rows = S // GRID_W
    nk = NA_SLAB_ROWS * GRID_W
    return pl.pallas_call(
        functools.partial(_na_kernel, n_rows=n_rows),
        grid=(4, B),
        in_specs=[pl.BlockSpec(memory_space=pltpu.SMEM),
                  pl.BlockSpec((1, 64, S), lambda h, b: (b, h, 0)),
                  pl.BlockSpec((1, 1, S, 128), lambda h, b: (b, h // 2, 0, 0)),
                  pl.BlockSpec((1, 64, S), lambda h, b: (b, h, 0)),
                  pl.BlockSpec((1, 1, 1, S), lambda h, b: (b, h, 0, 0)),
                  pl.BlockSpec((1, 3, nk, NA_Q_TILE), lambda h, b: (h, 0, 0, 0))],
        out_specs=pl.BlockSpec((1, 64, S), lambda h, b: (b, h, 0)),
        out_shape=jax.ShapeDtypeStruct((B, 256, S), F32),
        compiler_params=pltpu.CompilerParams(vmem_limit_bytes=VMEM_LIMIT),
        name="attn_nbr",
    )(bias_max, qT, k, vT, kn, bias_tab)


def _post_kernel(x_ref, oa_ref, ob_ref, od_ref, *rest, ff_chunk, n_oc, token_major_out):
    oc_refs = rest[:n_oc]
    (ga_ref, gb_ref, gd_ref, wout_ref, gpost_ref, gmlp_ref, wup_ref, wdown_ref, gmlp_post_ref,
     y_ref) = rest[n_oc:]
    cat = jnp.concatenate([
        _rms_rows(oa_ref[0], ga_ref[...]).astype(BF16),
        _rms_rows(ob_ref[0], gb_ref[...]).astype(BF16),
        *[oc_ref[0].astype(BF16) for oc_ref in oc_refs],
        _rms_rows(od_ref[0], gd_ref[...]).astype(BF16)], axis=0)
    x1 = x_ref[0] + _rms_rows(_dot(wout_ref[...], cat), gpost_ref[...])
    h = _rms_rows(x1, gmlp_ref[...]).astype(BF16)
    f = None
    for j in range(wup_ref.shape[0] // ff_chunk):
        up = _dot(wup_ref[j * ff_chunk:(j + 1) * ff_chunk, :], h)
        act = jnp.square(jnp.maximum(up, 0.0)).astype(BF16)
        down = _dot(wdown_ref[:, j * ff_chunk:(j + 1) * ff_chunk], act)
        f = down if f is None else f + down
    y = x1 + _rms_rows(f, gmlp_post_ref[...])
    y_ref[0] = y.T if token_major_out else y


def _post(xT, oa, ob, oc_parts, od, ga, gb, gd, woutT, gpost, gmlp, wupT, wdownT, gmlp_post,
          token_major_out):
    B, D, S = xT.shape
    tm = min(MLP_TOKEN_TILE, S)
    rows_spec = lambda rows: pl.BlockSpec((1, rows, tm), lambda b, i: (b, 0, i))
    consts = (ga, gb, gd, woutT, gpost, gmlp, wupT, wdownT, gmlp_post)
    out_spec, out_dims = rows_spec(D), (B, D, S)
    if token_major_out:
        out_spec, out_dims = pl.BlockSpec((1, tm, D), lambda b, i: (b, i, 0)), (B, S, D)
    return pl.pallas_call(
        functools.partial(_post_kernel, ff_chunk=MLP_FF_CHUNK, n_oc=len(oc_parts),
                          token_major_out=token_major_out),
        grid=(B, S // tm),
        in_specs=([rows_spec(D)] + [rows_spec(256)] * 3 + [rows_spec(o.shape[1]) for o in oc_parts]
                  + [_const_spec(c.shape) for c in consts]),
        out_specs=out_spec,
        out_shape=jax.ShapeDtypeStruct(out_dims, F32),
        compiler_params=pltpu.CompilerParams(vmem_limit_bytes=VMEM_LIMIT),
        name="outproj_mlp",
    )(xT, oa, ob, od, *oc_parts, *consts)


def _rope_tables(n_tokens):
    def angles(pos, dim):
        inv = ROPE_BASE ** (-jnp.arange(0, dim, 2, dtype=F32) / dim)
        return pos.astype(F32)[:, None] * inv[None, :]

    t = jnp.arange(n_tokens, dtype=jnp.int32)
    out = []
    for ang in (angles(t // GRID_W, HEAD_DIM // 2), angles(t % GRID_W, HEAD_DIM // 2),
                angles(t, MLA_ROPE)):
        out += [jnp.cos(ang).T, jnp.sin(ang).T]
    return out


def kernel(x, norm_mix_pre, norm_mix_post, norm_mlp_pre, norm_mlp_post, w_in, a_q_norm, a_k_norm,
           b_cq_norm, b_ckv_norm, b_w_uq, b_w_ukv, c_lambda_q1, c_lambda_k1, c_lambda_q2,
           c_lambda_k2, d_rel_bias, g_out_a, g_out_b, g_out_c, g_out_d, w_out, w_up, w_down):
    B, S, D = x.shape
    depth = w_in.shape[0]
    col = lambda v: jnp.broadcast_to(v.astype(F32).reshape(-1, 1), (v.size, LANES))
    wt = lambda w: w.T.astype(BF16)
    tables = _rope_tables(S)
    slopes = [2.0 ** (-8.0 * (hd + 1) / 4) for hd in range(4)]
    slopes2 = jnp.asarray(slopes, F32) * LOG2E
    tq, tk = min(Q_TILE, S), min(KV_CHUNK, S)
    tstep = tq * math.gcd(S // tq, Q_SUBTILES)
    reaches = [_alibi_reach(sl * LOG2E, S, tq, tk, tstep) for sl in slopes]
    head_groups = []
    for hd in range(4):
        if head_groups and reaches[hd] is None and head_groups[-1][1] is None:
            head_groups[-1][0].append(hd)
        else:
            head_groups.append(([hd], reaches[hd]))
    xT = x
    for l in range(depth):
        (qa, ka, va, qb, kb, vb, qc, kc, vc, qd, kd, vd, kna, knb, knc, knd, *x_feature_major) = _inproj(
            xT, col(norm_mix_pre[l]), wt(w_in[l]), col(a_q_norm[l]), col(a_k_norm[l]),
            col(b_cq_norm[l]), col(b_ckv_norm[l]), wt(b_w_uq[l]), wt(b_w_ukv[l]), tables,
            token_major=(l == 0))
        if l == 0:
            (xT,) = x_feature_major
        lambda_init = 0.8 - 0.6 * math.exp(-0.3 * l)
        lam_vecs = jnp.stack([c_lambda_q1[l], c_lambda_k1[l], c_lambda_q2[l], c_lambda_k2[l]]).astype(F32)
        oa = _attn_a(qa, ka, va, kna)
        ob = _attn_b(qb, kb, vb, knb)
        oc = [_attn_c(qc, kc, vc, knc, slopes2, lam_vecs, col(g_out_c[l]), lambda_init, tuple(hs), win)
              for hs, win in head_groups]
        rel_bias = d_rel_bias[l].astype(F32)
        od = _attn_d(qd, kd, vd, knd, _na_bias_tables(rel_bias, S // GRID_W),
                     jnp.max(rel_bias, axis=(1, 2)) * LOG2E)
        xT = _post(xT, oa, ob, oc, od, col(g_out_a[l]), col(g_out_b[l]), col(g_out_d[l]),
                   wt(w_out[l]), col(norm_mix_post[l]), col(norm_mlp_pre[l]), wt(w_up[l]),
                   wt(w_down[l]), col(norm_mlp_post[l]), token_major_out=(l == depth - 1))
    return xT
```

```python
import functools
import math

import numpy as np
import jax
import jax.numpy as jnp
from jax import lax
from jax.experimental import pallas as pl
from jax.experimental.pallas import tpu as pltpu

GRID_W = 64
HEAD_DIM = 64
N_KV_A = 2
MLA_Q_RANK = 192
MLA_KV_RANK = 128
MLA_NOPE = 64
MLA_ROPE = 32
DIFF_QK = 32
NA_ROWS = 8
NA_COLS = 16
ROPE_BASE = 10000.0
EPS = 1e-6
LOG2E = math.log2(math.e)
NEG_BIG = -1e30

A_COLS = 512
B_COLS = MLA_Q_RANK + MLA_KV_RANK + MLA_ROPE
C_COLS = 768
D_COLS = 768
OFF_B = A_COLS
OFF_C = OFF_B + B_COLS
OFF_D = OFF_C + C_COLS

TOKEN_TILE = 512
MLP_TOKEN_TILE = 512
Q_TILE = 512
Q_SUBTILES = 8
KV_CHUNK = 256
QK_LOOKAHEAD = 2
KV_UNROLL = 16
NA_Q_TILE = 256
NA_SLAB_ROWS = 12
NA_LOOKAHEAD = 2
VMEM_LIMIT = 56 * 1024 * 1024

F32 = jnp.float32
BF16 = jnp.bfloat16
LANES = 128


def _dot(a, b):
    return jnp.dot(a, b, preferred_element_type=F32)


def _rms_rows(x, gain):
    gain = jnp.concatenate([gain] * (x.shape[1] // LANES), axis=1)
    ms = jnp.mean(x * x, axis=0, keepdims=True)
    return x * lax.rsqrt(ms + EPS) * gain


def _rope_rows(x, cos, sin):
    n = cos.shape[0]
    x1, x2 = x[:n], x[n:]
    return x1 * cos - x2 * sin, x1 * sin + x2 * cos


def _inproj_kernel(x_ref, gpre_ref, w_ref, aq_ref, ak_ref, cqg_ref, ckvg_ref, wuq_ref, wukv_ref,
                   cr_ref, sr_ref, cc_ref, sc_ref, cs_ref, ss_ref,
                   qa_ref, ka_ref, va_ref, qb_ref, kb_ref, vb_ref,
                   qc_ref, kc_ref, vc_ref, qd_ref, kd_ref, vd_ref,
                   kna_ref, knb_ref, knc_ref, knd_ref, *xt_ref):
    if xt_ref:
        x = x_ref[0].T
        xt_ref[0][0] = x
    else:
        x = x_ref[0]
    h = _rms_rows(x, gpre_ref[...]).astype(BF16)
    cr, sr, cc, sc = cr_ref[...], sr_ref[...], cc_ref[...], sc_ref[...]
    cs, ss = cs_ref[...], ss_ref[...]
    tm = x.shape[1]

    def axial(t):
        r1, r2 = _rope_rows(t[0:32], cr, sr)
        c1, c2 = _rope_rows(t[32:64], cc, sc)
        return r1, r2, c1, c2

    def sumsq(t):
        return jnp.sum(t * t, axis=0, keepdims=True)

    ub = _dot(w_ref[OFF_B:OFF_B + B_COLS, :], h)
    ua = _dot(w_ref[0:A_COLS, :], h)
    uc = _dot(w_ref[OFF_C:OFF_C + C_COLS, :], h)
    ud = _dot(w_ref[OFF_D:OFF_D + D_COLS, :], h)
    cqn = _rms_rows(ub[0:MLA_Q_RANK], cqg_ref[...]).astype(BF16)
    ckvn = _rms_rows(ub[MLA_Q_RANK:MLA_Q_RANK + MLA_KV_RANK], ckvg_ref[...]).astype(BF16)
    qb = _dot(wuq_ref[...], cqn)
    kvb = _dot(wukv_ref[...], ckvn)

    qscale = HEAD_DIM ** -0.5 * LOG2E
    for hd in range(4):
        t = _rms_rows(ua[64 * hd:64 * hd + 64], aq_ref[...])
        for j, piece in enumerate(axial(t)):
            qa_ref[0, 64 * hd + 16 * j:64 * hd + 16 * j + 16, :] = (piece * qscale).astype(BF16)
    kparts = []
    for g in range(N_KV_A):
        t = _rms_rows(ua[256 + 64 * g:256 + 64 * g + 64], ak_ref[...])
        kparts.extend(axial(t))
        kna_ref[0, g] = sumsq(t)
    ka_ref[0] = jnp.concatenate(kparts, axis=0).T.astype(BF16)
    va_ref[0] = ua[384:512].astype(BF16)

    kr1, kr2 = _rope_rows(ub[MLA_Q_RANK + MLA_KV_RANK:B_COLS], cs, ss)
    bscale = (MLA_NOPE + MLA_ROPE) ** -0.5 * LOG2E
    zeros32 = jnp.zeros((32, tm), F32)
    for hd in range(4):
        base = 96 * hd
        q1, q2 = _rope_rows(qb[base + 64:base + 96], cs, ss)
        qb_ref[0, 128 * hd:128 * hd + 64, :] = (qb[base:base + 64] * bscale).astype(BF16)
        qb_ref[0, 128 * hd + 64:128 * hd + 80, :] = (q1 * bscale).astype(BF16)
        qb_ref[0, 128 * hd + 80:128 * hd + 96, :] = (q2 * bscale).astype(BF16)
        qb_ref[0, 128 * hd + 96:128 * hd + 128, :] = jnp.zeros((32, tm), BF16)
        kfull = jnp.concatenate([kvb[128 * hd:128 * hd + 64], kr1, kr2, zeros32], axis=0)
        kb_ref[0, hd] = kfull.T.astype(BF16)
        knb_ref[0, hd] = sumsq(kfull)
        vb_ref[0, 64 * hd:64 * hd + 64, :] = kvb[128 * hd + 64:128 * hd + 128].astype(BF16)

    qc_ref[0] = (uc[0:256] * (DIFF_QK ** -0.5 * LOG2E)).astype(BF16)
    for p in range(2):
        kc_ref[0, p] = uc[256 + 128 * p:256 + 128 * p + 128].T.astype(BF16)
    vc_ref[0] = uc[512:768].astype(BF16)
    for hd in range(4):
        for mp in range(2):
            base = 256 + 64 * hd + 32 * mp
            knc_ref[0, hd, mp:mp + 1, :] = sumsq(uc[base:base + 32])

    qd_ref[0] = (ud[0:256] * (HEAD_DIM ** -0.5 * LOG2E)).astype(BF16)
    for p in range(2):
        kd_ref[0, p] = ud[256 + 128 * p:256 + 128 * p + 128].T.astype(BF16)
    vd_ref[0] = ud[512:768].astype(BF16)
    for hd in range(4):
        knd_ref[0, hd] = sumsq(ud[256 + 64 * hd:256 + 64 * hd + 64])


def _const_spec(shape):
    nd = len(shape)
    return pl.BlockSpec(shape, lambda *_: (0,) * nd, pipeline_mode=pl.Buffered(1))


def _inproj(x, gpre, wT, aq, ak, cqg, ckvg, wuqT, wukvT, tables, token_major):
    B, D, S = (x.shape[0], x.shape[2], x.shape[1]) if token_major else x.shape
    tm = min(TOKEN_TILE, S)
    grid = (B, S // tm)
    tab_spec = pl.BlockSpec((16, tm), lambda b, i: (0, i))
    rows_spec = lambda rows: pl.BlockSpec((1, rows, tm), lambda b, i: (b, 0, i))
    out_shape = (
        jax.ShapeDtypeStruct((B, 256, S), BF16),
        jax.ShapeDtypeStruct((B, S, 128), BF16),
        jax.ShapeDtypeStruct((B, 128, S), BF16),
        jax.ShapeDtypeStruct((B, 512, S), BF16),
        jax.ShapeDtypeStruct((B, 4, S, 128), BF16),
        jax.ShapeDtypeStruct((B, 256, S), BF16),
        jax.ShapeDtypeStruct((B, 256, S), BF16),
        jax.ShapeDtypeStruct((B, 2, S, 128), BF16),
        jax.ShapeDtypeStruct((B, 256, S), BF16),
        jax.ShapeDtypeStruct((B, 256, S), BF16),
        jax.ShapeDtypeStruct((B, 2, S, 128), BF16),
        jax.ShapeDtypeStruct((B, 256, S), BF16),
        jax.ShapeDtypeStruct((B, 2, 1, S), F32),
        jax.ShapeDtypeStruct((B, 4, 1, S), F32),
        jax.ShapeDtypeStruct((B, 4, 2, S), F32),
        jax.ShapeDtypeStruct((B, 4, 1, S), F32),
    )
    kspec = lambda n: pl.BlockSpec((1, n, tm, 128), lambda b, i: (b, 0, i, 0))
    nspec = lambda n, r: pl.BlockSpec((1, n, r, tm), lambda b, i: (b, 0, 0, i))
    out_specs = (
        rows_spec(256), pl.BlockSpec((1, tm, 128), lambda b, i: (b, i, 0)), rows_spec(128),
        rows_spec(512), kspec(4), rows_spec(256),
        rows_spec(256), kspec(2), rows_spec(256),
        rows_spec(256), kspec(2), rows_spec(256),
        nspec(2, 1), nspec(4, 1), nspec(4, 2), nspec(4, 1),
    )
    x_spec = rows_spec(D)
    if token_major:
        x_spec = pl.BlockSpec((1, tm, D), lambda b, i: (b, i, 0))
        out_shape += (jax.ShapeDtypeStruct((B, D, S), F32),)
        out_specs += (rows_spec(D),)
    in_specs = [
        x_spec, _const_spec(gpre.shape), _const_spec(wT.shape),
        _const_spec(aq.shape), _const_spec(ak.shape), _const_spec(cqg.shape), _const_spec(ckvg.shape),
        _const_spec(wuqT.shape), _const_spec(wukvT.shape),
    ] + [tab_spec] * 6
    return pl.pallas_call(
        _inproj_kernel, grid=grid, in_specs=in_specs, out_specs=out_specs, out_shape=out_shape,
        compiler_params=pltpu.CompilerParams(vmem_limit_bytes=VMEM_LIMIT),
        name="inproj",
    )(x, gpre, wT, aq, ak, cqg, ckvg, wuqT, wukvT, *tables)


def _pad_rows(q32, slot, nslots):
    return jnp.concatenate([jnp.where(slot == j, q32, 0.0) for j in range(nslots)], axis=0)


def _softmax_stage(t, cst, m):
    m_new = jnp.maximum(m, jnp.max(t, axis=0, keepdims=True) + cst)
    alpha = jnp.exp2(m - m_new)
    p = jnp.exp2(t - (m_new - cst))
    return m_new, alpha, p


def _fold_rows(p):
    slabs = [p[r:r + 8] for r in range(0, p.shape[0], 8)]
    chains = [functools.reduce(jnp.add, slabs[w::4]) for w in range(min(4, len(slabs)))]
    return functools.reduce(jnp.add, chains)


SHIFT_OK_MIN_SUM = 2.0 ** -64
ZERO_PROB_LOG2 = 140.0


def _attention(nchunks, tq, n_sub, n_maps, shift_of, qk_stage, pv_stage, finalize, window=None):
    dv = pv_stage.rows
    zero_acc = lambda: tuple(jnp.zeros((dv, tq), F32) for _ in range(n_maps))
    zero_sum = lambda: tuple(jnp.zeros((8, tq), F32) for _ in range(n_maps))

    lookahead = QK_LOOKAHEAD + (1 if n_maps == 1 else 0)

    def run_tasks(tasks, accs, sums):
        queue = []
        for idx, (sub, c, j) in enumerate(tasks):
            while len(queue) + idx < min(idx + lookahead + 1, len(tasks)):
                ahead_sub, ahead_c, ahead_j = tasks[idx + len(queue)]
                queue.append(qk_stage(ahead_c, ahead_j, ahead_sub))
            tile, cst = queue.pop(0)
            p = jnp.exp2(tile - (shift_of(sub, j) - cst))
            sums[sub][j] = sums[sub][j] + _fold_rows(p)
            accs[sub][j] = accs[sub][j] + pv_stage(c, p.astype(BF16))

    unroll = math.gcd(nchunks, KV_UNROLL)
    accs = [list(zero_acc()) for _ in range(n_sub)]
    sums = [list(zero_sum()) for _ in range(n_sub)]
    if window is not None:
        run_tasks([(sub, window[sub][0] + w, j) for sub in range(n_sub) for w in range(window[sub][1])
                   for j in range(n_maps)], accs, sums)
    elif unroll == nchunks:
        run_tasks([(sub, c, j) for sub in range(n_sub) for c in range(nchunks)
                   for j in range(n_maps)], accs, sums)
    else:
        for sub in range(n_sub):
            for j in range(n_maps):
                shift_of(sub, j)

            def trip(t, carry, sub=sub):
                acc, ssum = {sub: list(carry[0])}, {sub: list(carry[1])}
                run_tasks([(sub, t * unroll + u, j) for u in range(unroll) for j in range(n_maps)],
                          acc, ssum)
                return tuple(acc[sub]), tuple(ssum[sub])
            done = lax.fori_loop(0, nchunks // unroll, trip, (tuple(accs[sub]), tuple(sums[sub])))
            accs[sub], sums[sub] = list(done[0]), list(done[1])

    lmins = []
    for sub in range(n_sub):
        denom = [jnp.sum(s8, axis=0, keepdims=True) for s8 in sums[sub]]
        finalize(sub, list(zip(accs[sub], denom)))
        lmins += [jnp.min(d) for d in denom]

    @pl.when(jnp.logical_not(functools.reduce(jnp.minimum, lmins) >= SHIFT_OK_MIN_SUM))
    def _():
        for sub in range(n_sub):
            def online_trip(c, carry, sub=sub):
                out = []
                for j, (m, l, acc) in enumerate(carry):
                    tile, cst = qk_stage(c, j, sub)
                    m_new, alpha, p = _softmax_stage(tile, cst, m)
                    out.append((m_new, alpha * l + jnp.sum(p, axis=0, keepdims=True),
                                alpha * acc + pv_stage(c, p.astype(BF16))))
                return tuple(out)

            init = tuple((jnp.full((1, tq), NEG_BIG, F32), jnp.zeros((1, tq), F32), acc0)
                         for acc0 in zero_acc())
            done = lax.fori_loop(0, nchunks, online_trip, init)
            finalize(sub, [(acc, l) for _, l, acc in done])


def _chunk_slice(c, tk):
    return pl.ds(c * tk if isinstance(c, int) else pl.multiple_of(c * tk, tk), tk)


def _make_pv_stage(v_ref, tk):
    def pv_stage(c, p):
        return _dot(v_ref[0, :, _chunk_slice(c, tk)], p)

    pv_stage.rows = v_ref.shape[1]
    return pv_stage


BOUND_SLACK = 1.01


def _score_bound(q32, kn2):
    return BOUND_SLACK * jnp.sqrt(jnp.sum(q32 * q32, axis=0, keepdims=True)
                                  * jnp.max(kn2, axis=1, keepdims=True))


def _attn_kernel(q_ref, k_ref, v_ref, kn_ref, o_ref, *, k_slots, nchunks, tk, tq):
    n_sub = q_ref.shape[2] // tq

    @functools.cache
    def queries(sub):
        q32 = q_ref[0, :, sub * tq:(sub + 1) * tq].astype(F32)
        shift = _score_bound(q32, kn_ref[0, 0])
        if k_slots > 1:
            q32 = _pad_rows(q32, pl.program_id(1) // 2, k_slots)
        return q32.astype(BF16), shift

    k_at = (lambda sl: k_ref[0, sl, :]) if len(k_ref.shape) == 3 else (lambda sl: k_ref[0, 0, sl, :])

    def qk_stage(c, j, sub):
        return _dot(k_at(_chunk_slice(c, tk)), queries(sub)[0]), jnp.float32(0.0)

    def finalize(sub, acc):
        (o, l), = acc
        o_ref[0, :, sub * tq:(sub + 1) * tq] = o / l

    _attention(nchunks, tq, n_sub, 1, lambda sub, j: queries(sub)[1], qk_stage,
               _make_pv_stage(v_ref, tk), finalize)


def _attn_a(qT, k, vT, kn):
    B, _, S = qT.shape
    tq, tk = min(Q_TILE, S), min(KV_CHUNK, S)
    tstep = tq * math.gcd(S // tq, Q_SUBTILES)
    return pl.pallas_call(
        functools.partial(_attn_kernel, k_slots=2, nchunks=S // tk, tk=tk, tq=tq),
        grid=(B, 4, S // tstep),
        in_specs=[pl.BlockSpec((1, 64, tstep), lambda b, h, i: (b, h, i)),
                  pl.BlockSpec((1, S, 128), lambda b, h, i: (b, 0, 0)),
                  pl.BlockSpec((1, 64, S), lambda b, h, i: (b, h // 2, 0)),
                  pl.BlockSpec((1, 1, 1, S), lambda b, h, i: (b, h // 2, 0, 0))],
        out_specs=pl.BlockSpec((1, 64, tstep), lambda b, h, i: (b, h, i)),
        out_shape=jax.ShapeDtypeStruct((B, 256, S), F32),
        compiler_params=pltpu.CompilerParams(vmem_limit_bytes=VMEM_LIMIT),
        name="attn_gqa",
    )(qT, k, vT, kn)


def _attn_b(qT, k, vT, kn):
    B, _, S = qT.shape
    tq, tk = min(Q_TILE, S), min(KV_CHUNK, S)
    tstep = tq * math.gcd(S // tq, Q_SUBTILES)
    return pl.pallas_call(
        functools.partial(_attn_kernel, k_slots=1, nchunks=S // tk, tk=tk, tq=tq),
        grid=(B, 4, S // tstep),
        in_specs=[pl.BlockSpec((1, 128, tstep), lambda b, h, i: (b, h, i)),
                  pl.BlockSpec((1, 1, S, 128), lambda b, h, i: (b, h, 0, 0)),
                  pl.BlockSpec((1, 64, S), lambda b, h, i: (b, h, 0)),
                  pl.BlockSpec((1, 1, 1, S), lambda b, h, i: (b, h, 0, 0))],
        out_specs=pl.BlockSpec((1, 64, tstep), lambda b, h, i: (b, h, i)),
        out_shape=jax.ShapeDtypeStruct((B, 256, S), F32),
        compiler_params=pltpu.CompilerParams(vmem_limit_bytes=VMEM_LIMIT),
        name="attn_mla",
    )(qT, k, vT, kn)


def _diff_kernel(slope_ref, lam_ref, q_ref, k_ref, v_ref, kn_ref, g_ref, o_ref, bias_ref,
                 *, nchunks, tk, tq, head0, reach, out_scale, lambda_init):
    h = head0 + pl.program_id(1)
    n_sub = q_ref.shape[2] // tq
    one_step = n_sub * tq == nchunks * tk
    i = 0 if one_step else pl.program_id(2)
    dv = v_ref.shape[1]
    n_inside = tq // tk
    slope = slope_ref[h]

    @pl.when(pl.program_id(2) == 0)
    def _():
        d = (lax.broadcasted_iota(jnp.int32, (tk, tq), 1)
             - lax.broadcasted_iota(jnp.int32, (tk, tq), 0)).astype(F32)
        bias_ref[0] = -d * slope
        bias_ref[1] = d * slope
        for j in range(n_inside):
            bias_ref[2 + j] = -jnp.abs(d - float(j * tk)) * slope

    slot0 = (h % 2) * 2

    @functools.cache
    def queries(sub, mp):
        q32 = q_ref[0, 32 * mp:32 * mp + 32, sub * tq:(sub + 1) * tq].astype(F32)
        shift = _score_bound(q32, kn_ref[0, 0, mp:mp + 1, :])
        return _pad_rows(q32, slot0 + mp, 4).astype(BF16), shift

    def alibi_case(gap):
        if isinstance(gap, int):
            if gap > 0:
                return 0, -gap * slope
            return (2 + (-gap) // tk, jnp.float32(0.0)) if gap > -tq else (1, gap * slope)
        inside = jnp.logical_and(gap <= 0, gap > -tq)
        return (jnp.where(gap > 0, 0, jnp.where(inside, 2 + (-gap) // tk, 1)),
                jnp.where(inside, 0.0, -jnp.abs(gap).astype(F32) * slope))

    def qk_stage(c, mp, sub):
        kc = k_ref[0, 0, _chunk_slice(c, tk), :]
        table, cst = alibi_case((i * n_sub + sub) * tq - c * tk)
        return _dot(kc, queries(sub, mp)[0]) + bias_ref[table], cst

    def finalize(sub, acc):
        (a1, l1), (a2, l2) = acc
        lv = lam_ref[...]
        lam = (jnp.exp(jnp.sum(lv[0:1] * lv[1:2], axis=1, keepdims=True))
               - jnp.exp(jnp.sum(lv[2:3] * lv[3:4], axis=1, keepdims=True)) + lambda_init)
        o = a1 / l1 - lam * (a2 / l2)
        o_ref[0, :, sub * tq:(sub + 1) * tq] = _rms_rows(o, g_ref[...]) * out_scale

    window = None
    if reach is not None:
        window = _alibi_windows(reach, nchunks, tk, tq, n_sub, None if one_step else i)
    _attention(nchunks, tq, n_sub, 2, lambda sub, mp: queries(sub, mp)[1], qk_stage,
               _make_pv_stage(v_ref, tk), finalize, window)


def _alibi_windows(reach, nchunks, tk, tq, n_sub, step):
    if step is None:
        out = []
        for sub in range(n_sub):
            lo = max(sub * tq - reach, 0) // tk
            hi = min(sub * tq + tq - 1 + reach, nchunks * tk - 1) // tk
            out.append((lo, hi - lo + 1))
        return out
    n_win = min(-(-(tq + 2 * reach) // tk) + 1, nchunks)
    for q0 in range(0, nchunks * tk, tq):
        first = min(max((q0 - reach) // tk, 0), nchunks - n_win)
        lo, hi = max(q0 - reach, 0) // tk, min(q0 + tq - 1 + reach, nchunks * tk - 1) // tk
        assert first <= lo and hi < first + n_win
    return [(jnp.clip(((step * n_sub + sub) * tq - reach) // tk, 0, nchunks - n_win), n_win)
            for sub in range(n_sub)]


def _alibi_reach(slope2, n_tokens, tq, tk, tstep):
    reach = math.ceil(ZERO_PROB_LOG2 / slope2)
    nchunks = n_tokens // tk
    if tstep == n_tokens:
        counts = [n for _, n in _alibi_windows(reach, nchunks, tk, tq, n_tokens // tq, None)]
    else:
        counts = [min(-(-(tq + 2 * reach) // tk) + 1, nchunks)]
    return reach if min(counts) < nchunks else None


def _attn_c(qT, k, vT, kn, slopes2, lam_vecs, g_head, lambda_init, heads, reach):
    B, _, S = qT.shape
    tq, tk = min(Q_TILE, S), min(KV_CHUNK, S)
    tstep = tq * math.gcd(S // tq, Q_SUBTILES)
    smem = pl.BlockSpec(memory_space=pltpu.SMEM)
    h0 = heads[0]
    return pl.pallas_call(
        functools.partial(_diff_kernel, nchunks=S // tk, tk=tk, tq=tq, head0=h0, reach=reach,
                          out_scale=1.0 - lambda_init, lambda_init=lambda_init),
        grid=(B, len(heads), S // tstep),
        in_specs=[smem,
                  pl.BlockSpec((4, 32), lambda b, h, i: (0, 0)),
                  pl.BlockSpec((1, 64, tstep), lambda b, h, i: (b, h0 + h, i)),
                  pl.BlockSpec((1, 1, S, 128), lambda b, h, i: (b, (h0 + h) // 2, 0, 0)),
                  pl.BlockSpec((1, 64, S), lambda b, h, i: (b, h0 + h, 0)),
                  pl.BlockSpec((1, 1, 2, S), lambda b, h, i: (b, h0 + h, 0, 0)),
                  pl.BlockSpec(g_head.shape, lambda b, h, i: (0, 0))],
        out_specs=pl.BlockSpec((1, 64, tstep), lambda b, h, i: (b, h, i)),
        out_shape=jax.ShapeDtypeStruct((B, 64 * len(heads), S), F32),
        scratch_shapes=[pltpu.VMEM((2 + tq // tk, tk, tq), F32)],
        compiler_params=pltpu.CompilerParams(vmem_limit_bytes=VMEM_LIMIT),
        name="attn_diff",
    )(slopes2, lam_vecs, qT, k, vT, kn, g_head)


def _na_kernel(bmax_ref, q_ref, k_ref, v_ref, kn_ref, bias_ref, o_ref, *, n_rows):
    h = pl.program_id(0)
    n_blocks = n_rows // 4
    nk = NA_SLAB_ROWS * GRID_W
    tq = NA_Q_TILE
    dv = v_ref.shape[1]
    ones = jnp.ones((16, nk), BF16)
    kmax2 = jnp.max(kn_ref[0, 0], axis=1, keepdims=True)
    bmax = bmax_ref[h]

    def variant(a):
        return 0 if a == 0 else (2 if a == n_blocks - 1 else 1)

    def slab(a):
        return min(max(4 * a - 4, 0), n_rows - NA_SLAB_ROWS) * GRID_W

    def q_block(a):
        return q_ref[0, :, a * tq:(a + 1) * tq].astype(F32)

    def scores(a):
        qp = _pad_rows(q_block(a), h % 2, 2).astype(BF16)
        return _dot(k_ref[0, 0, slab(a):slab(a) + nk, :], qp) + bias_ref[0, variant(a)]

    def pv(a, p):
        vaug = jnp.concatenate([v_ref[0, :, slab(a):slab(a) + nk], ones], axis=0)
        return _dot(vaug, p)

    lmin = None
    queue = []
    for a in range(n_blocks):
        while len(queue) + a < min(a + NA_LOOKAHEAD + 1, n_blocks):
            queue.append(scores(a + len(queue)))
        s = queue.pop(0)
        q32 = q_block(a)
        shift = jnp.sqrt(jnp.sum(q32 * q32, axis=0, keepdims=True) * kmax2) + bmax
        acc = pv(a, jnp.exp2(s - shift).astype(BF16))
        o_ref[0, :, a * tq:(a + 1) * tq] = acc[0:dv] / acc[dv:dv + 1]
        amin = jnp.min(acc[dv:dv + 1])
        lmin = amin if lmin is None else jnp.minimum(lmin, amin)

    @pl.when(jnp.logical_not(lmin >= SHIFT_OK_MIN_SUM))
    def _():
        for a in range(n_blocks):
            s = scores(a)
            p = jnp.exp2(s - jnp.max(s, axis=0, keepdims=True)).astype(BF16)
            acc = pv(a, p)
            o_ref[0, :, a * tq:(a + 1) * tq] = acc[0:dv] / acc[dv:dv + 1]


def _na_bias_tables(rel_bias, n_rows):
    n_blocks = n_rows // 4
    kr_w = min(NA_ROWS, n_rows)
    col = np.arange(GRID_W)
    c0 = np.clip(col - NA_COLS // 2, 0, GRID_W - NA_COLS)
    col_ok = (col[:, None] >= c0[None, :]) & (col[:, None] < c0[None, :] + NA_COLS)
    dc = col[:, None] - col[None, :] + (NA_COLS - 1)
    onehot_c = ((dc[None] == np.arange(2 * NA_COLS - 1)[:, None, None]) & col_ok[None]).astype(np.float32)

    def row_geometry(a):
        R0 = int(np.clip(4 * a - 4, 0, n_rows - NA_SLAB_ROWS))
        r = 4 * a + np.arange(4)
        r0 = np.clip(r - kr_w // 2, 0, n_rows - kr_w)
        kr = R0 + np.arange(NA_SLAB_ROWS)
        ok = (kr[:, None] >= r0[None, :]) & (kr[:, None] < r0[None, :] + kr_w)
        dr = kr[:, None] - r[None, :] + (NA_ROWS - 1)
        return ((dr[:, :, None] == np.arange(2 * NA_ROWS - 1)) & ok[:, :, None]), ok

    variants = [row_geometry(0), row_geometry(1), row_geometry(n_blocks - 1)]
    for a in range(1, n_blocks - 1):
        assert np.array_equal(row_geometry(a)[0], variants[1][0])
    onehot_r = np.stack([v[0] for v in variants]).astype(np.float32)
    valid = (np.stack([v[1] for v in variants])[:, :, None, :, None]
             & col_ok[None, None, :, None, :])
    assert (valid.reshape(3, -1, NA_Q_TILE).sum(axis=1) == kr_w * NA_COLS).all()
    hi = lax.Precision.HIGHEST
    t1 = jnp.einsum('hrd,dkc->hrkc', rel_bias, onehot_c, precision=hi)
    tab = jnp.einsum('viqr,hrkc->hvikqc', onehot_r, t1, precision=hi) * LOG2E
    tab = jnp.where(valid[None], tab, NEG_BIG)
    return tab.reshape(4, 3, NA_SLAB_ROWS * GRID_W, NA_Q_TILE).astype(F32)


def _attn_d(qT, k, vT, kn, bias_tab, bias_max):
    B, _, S = qT.shape
    n_rows = S // GRID_W
    nk = NA_SLAB_ROWS * GRID_W
    return pl.pallas_call(
        functools.partial(_na_kernel, n_rows=n_rows),
        grid=(4, B),
        in_specs=[pl.BlockSpec(memory_space=pltpu.SMEM),
                  pl.BlockSpec((1, 64, S), lambda h, b: (b, h, 0)),
                  pl.BlockSpec((1, 1, S, 128), lambda h, b: (b, h // 2, 0, 0)),
                  pl.BlockSpec((1, 64, S), lambda h, b: (b, h, 0)),
                  pl.BlockSpec((1, 1, 1, S), lambda h, b: (b, h, 0, 0)),
                  pl.BlockSpec((1, 3, nk, NA_Q_TILE), lambda h, b: (h, 0, 0, 0))],
        out_specs=pl.BlockSpec((1, 64, S), lambda h, b: (b, h, 0)),
        out_shape=jax.ShapeDtypeStruct((B, 256, S), F32),
        compiler_params=pltpu.CompilerParams(vmem_limit_bytes=VMEM_LIMIT),
        name="attn_nbr",
    )(bias_max, qT, k, vT, kn, bias_tab)


def _post_kernel(x_ref, oa_ref, ob_ref, od_ref, *rest, n_oc, token_major_out):
    oc_refs = rest[:n_oc]
    (ga_ref, gb_ref, gd_ref, wout_ref, gpost_ref, gmlp_ref, wup_ref, wdown_ref, gmlp_post_ref,
     y_ref) = rest[n_oc:]
    cat = jnp.concatenate([
        _rms_rows(oa_ref[0], ga_ref[...]).astype(BF16),
        _rms_rows(ob_ref[0], gb_ref[...]).astype(BF16),
        *[oc_ref[0].astype(BF16) for oc_ref in oc_refs],
        _rms_rows(od_ref[0], gd_ref[...]).astype(BF16)], axis=0)
    x1 = x_ref[0] + _rms_rows(_dot(wout_ref[...], cat), gpost_ref[...])
    h = _rms_rows(x1, gmlp_ref[...]).astype(BF16)
    up = _dot(wup_ref[...], h)
    f = _dot(wdown_ref[...], jnp.square(jnp.maximum(up, 0.0)).astype(BF16))
    y = x1 + _rms_rows(f, gmlp_post_ref[...])
    y_ref[0] = y.T if token_major_out else y


def _post(xT, oa, ob, oc_parts, od, ga, gb, gd, woutT, gpost, gmlp, wupT, wdownT, gmlp_post,
          token_major_out):
    B, D, S = xT.shape
    tm = min(MLP_TOKEN_TILE, S)
    rows_spec = lambda rows: pl.BlockSpec((1, rows, tm), lambda b, i: (b, 0, i))
    consts = (ga, gb, gd, woutT, gpost, gmlp, wupT, wdownT, gmlp_post)
    out_spec, out_dims = rows_spec(D), (B, D, S)
    if token_major_out:
        out_spec, out_dims = pl.BlockSpec((1, tm, D), lambda b, i: (b, i, 0)), (B, S, D)
    return pl.pallas_call(
        functools.partial(_post_kernel, n_oc=len(oc_parts), token_major_out=token_major_out),
        grid=(B, S // tm),
        in_specs=([rows_spec(D)] + [rows_spec(256)] * 3 + [rows_spec(o.shape[1]) for o in oc_parts]
                  + [_const_spec(c.shape) for c in consts]),
        out_specs=out_spec,
        out_shape=jax.ShapeDtypeStruct(out_dims, F32),
        compiler_params=pltpu.CompilerParams(vmem_limit_bytes=VMEM_LIMIT),
        name="outproj_mlp",
    )(xT, oa, ob, od, *oc_parts, *consts)


def _rope_tables(n_tokens):
    def angles(pos, dim):
        inv = ROPE_BASE ** (-jnp.arange(0, dim, 2, dtype=F32) / dim)
        return pos.astype(F32)[:, None] * inv[None, :]

    t = jnp.arange(n_tokens, dtype=jnp.int32)
    out = []
    for ang in (angles(t // GRID_W, HEAD_DIM // 2), angles(t % GRID_W, HEAD_DIM // 2),
                angles(t, MLA_ROPE)):
        out += [jnp.cos(ang).T, jnp.sin(ang).T]
    return out


def kernel(x, norm_mix_pre, norm_mix_post, norm_mlp_pre, norm_mlp_post, w_in, a_q_norm, a_k_norm,
           b_cq_norm, b_ckv_norm, b_w_uq, b_w_ukv, c_lambda_q1, c_lambda_k1, c_lambda_q2,
           c_lambda_k2, d_rel_bias, g_out_a, g_out_b, g_out_c, g_out_d, w_out, w_up, w_down):
    B, S, D = x.shape
    depth = w_in.shape[0]
    col = lambda v: jnp.broadcast_to(v.astype(F32).reshape(-1, 1), (v.size, LANES))
    wt = lambda w: w.T.astype(BF16)
    tables = _rope_tables(S)
    slopes = [2.0 ** (-8.0 * (hd + 1) / 4) for hd in range(4)]
    slopes2 = jnp.asarray(slopes, F32) * LOG2E
    tq, tk = min(Q_TILE, S), min(KV_CHUNK, S)
    tstep = tq * math.gcd(S // tq, Q_SUBTILES)
    reaches = [_alibi_reach(sl * LOG2E, S, tq, tk, tstep) for sl in slopes]
    head_groups = []
    for hd in range(4):
        if head_groups and reaches[hd] is None and head_groups[-1][1] is None:
            head_groups[-1][0].append(hd)
        else:
            head_groups.append(([hd], reaches[hd]))
    xT = x
    for l in range(depth):
        (qa, ka, va, qb, kb, vb, qc, kc, vc, qd, kd, vd, kna, knb, knc, knd, *x_feature_major) = _inproj(
            xT, col(norm_mix_pre[l]), wt(w_in[l]), col(a_q_norm[l]), col(a_k_norm[l]),
            col(b_cq_norm[l]), col(b_ckv_norm[l]), wt(b_w_uq[l]), wt(b_w_ukv[l]), tables,
            token_major=(l == 0))
        if l == 0:
            (xT,) = x_feature_major
        lambda_init = 0.8 - 0.6 * math.exp(-0.3 * l)
        lam_vecs = jnp.stack([c_lambda_q1[l], c_lambda_k1[l], c_lambda_q2[l], c_lambda_k2[l]]).astype(F32)
        oa = _attn_a(qa, ka, va, kna)
        ob = _attn_b(qb, kb, vb, knb)
        oc = [_attn_c(qc, kc, vc, knc, slopes2, lam_vecs, col(g_out_c[l]), lambda_init, tuple(hs), win)
              for hs, win in head_groups]
        rel_bias = d_rel_bias[l].astype(F32)
        od = _attn_d(qd, kd, vd, knd, _na_bias_tables(rel_bias, S // GRID_W),
                     jnp.max(rel_bias, axis=(1, 2)) * LOG2E)
        xT = _post(xT, oa, ob, oc, od, col(g_out_a[l]), col(g_out_b[l]), col(g_out_d[l]),
                   wt(w_out[l]), col(norm_mix_post[l]), col(norm_mlp_pre[l]), wt(w_up[l]),
                   wt(w_down[l]), col(norm_mlp_post[l]), token_major_out=(l == depth - 1))
    return xT
```
